```python
import math
import jax, jax.numpy as jnp
from jax import lax
import numpy as np

D_MODEL = 1024
BATCH = 8
SEQ = 4096
DEPTH = 1

H_A = 8
DK_A = 64
DV_A = 64
W_A = H_A * DV_A
CONV_K = 4
CHUNK = 64
H_B = 8
DH_B = 64
W_B = H_B * DH_B
HI = 8
DI = 64
TOPK_MAX = 256
Q_BLOCK = 128
ROT_DIM = DH_B // 4
ROPE_THETA = 500000.0
D_FF = 2816
PLE_DIM = 256
EPS = 1e-6

MIX_SIZES = (W_A, W_A, W_A, W_A, H_A, H_A,
             W_B, W_B, W_B, HI * DI, DI, HI,
             D_MODEL, D_MODEL)
MIX_COLS = sum(MIX_SIZES)

kernel_name = "hybrid_gdn_dsa_macaron_layer"


def rmsnorm(x, g):
    x32 = x.astype(jnp.float32)
    y = x32 * lax.rsqrt(jnp.mean(x32 * x32, axis=-1, keepdims=True) + EPS)
    return (y * g.astype(jnp.float32)).astype(x.dtype)


def l2norm(x):
    x32 = x.astype(jnp.float32)
    return x32 * lax.rsqrt(jnp.sum(x32 * x32, axis=-1, keepdims=True) + EPS)


def swiglu(u, w_in, w_out):
    gate, up = jnp.split(u @ w_in, 2, axis=-1)
    return (jax.nn.silu(gate) * up) @ w_out


def rope_tables(seq_len):
    pos = jnp.arange(seq_len, dtype=jnp.float32)
    inv = ROPE_THETA ** (-jnp.arange(0, ROT_DIM, 2, dtype=jnp.float32) / ROT_DIM)
    ang = pos[:, None] * inv[None, :]
    return jnp.cos(ang), jnp.sin(ang)


def partial_rope(x, cos, sin):
    c = cos[None, :, None, :].astype(x.dtype)
    s = sin[None, :, None, :].astype(x.dtype)
    half = ROT_DIM // 2
    x1, x2, xp = x[..., :half], x[..., half:ROT_DIM], x[..., ROT_DIM:]
    return jnp.concatenate([x1 * c - x2 * s, x2 * c + x1 * s, xp], axis=-1)


def causal_conv(x, w):
    k = w.shape[0]
    return lax.conv_general_dilated(
        x, w[:, None, :], window_strides=(1,), padding=[(k - 1, 0)],
        dimension_numbers=("NWC", "WIO", "NWC"), feature_group_count=x.shape[-1])


def gated_delta_rule(q, k, v, g, beta):
    B, S, H, DK = q.shape
    DV = v.shape[-1]
    N, C = S // CHUNK, CHUNK

    def chunks(t):
        return t.reshape(B, N, C, H, -1).transpose(0, 3, 1, 2, 4)

    q, k, v = chunks(q), chunks(k), chunks(v)
    g = g.reshape(B, N, C, H).transpose(0, 3, 1, 2)
    beta = beta.reshape(B, N, C, H).transpose(0, 3, 1, 2)
    gc = jnp.cumsum(g, axis=-1)
    idx = jnp.arange(C)
    incl = idx[:, None] >= idx[None, :]
    strict = idx[:, None] > idx[None, :]
    decay = jnp.exp(jnp.where(incl, gc[..., :, None] - gc[..., None, :], -jnp.inf))
    kk = jnp.einsum("bhnid,bhnjd->bhnij", k, k)
    a_mat = jnp.where(strict, beta[..., :, None] * kk * decay, 0.0) + jnp.eye(C, dtype=jnp.float32)
    rhs = jnp.concatenate([v * beta[..., None], k * (beta * jnp.exp(gc))[..., None]], axis=-1)
    sol = lax.linalg.triangular_solve(a_mat, rhs, left_side=True, lower=True, unit_diagonal=True)
    u_val, w_dec = sol[..., :DV], sol[..., DV:]
    qk = jnp.einsum("bhnid,bhnjd->bhnij", q, k) * decay
    q_dec = q * jnp.exp(gc)[..., None]
    k_dec = k * jnp.exp(gc[..., -1:] - gc)[..., None]
    g_tot = jnp.exp(gc[..., -1])

    def step(state, inp):
        u_c, w_c, qk_c, qd_c, kd_c, gt_c = inp
        v_new = u_c - jnp.einsum("bhcd,bhde->bhce", w_c, state)
        o = jnp.einsum("bhcd,bhde->bhce", qd_c, state) + jnp.einsum("bhij,bhje->bhie", qk_c, v_new)
        state = state * gt_c[..., None, None] + jnp.einsum("bhcd,bhce->bhde", kd_c, v_new)
        return state, o

    xs = tuple(jnp.moveaxis(t, 2, 0) for t in (u_val, w_dec, qk, q_dec, k_dec, g_tot))
    s0 = jnp.zeros((B, H, DK, DV), jnp.float32)
    _, o = lax.scan(step, s0, xs)
    return o.transpose(1, 0, 3, 2, 4).reshape(B, S, H, DV)


def dsa_attention(q, k, v, q_idx, k_idx, w_idx):
    B, S, H, Dh = q.shape
    n_blk = S // Q_BLOCK
    top = min(TOPK_MAX, S // 4)
    key_pos = jnp.arange(S)
    gather = jax.vmap(lambda t, i: t[i])

    def to_blocks(t):
        return jnp.moveaxis(t.reshape(B, n_blk, Q_BLOCK, *t.shape[2:]), 1, 0)

    def one_block(args):
        qb, qib, wb, start = args
        qpos = start + jnp.arange(Q_BLOCK)
        causal = key_pos[None, :] <= qpos[:, None]
        dots = jnp.einsum("bqhd,bsd->bqhs", qib, k_idx).astype(jnp.float32) * (DI ** -0.5)
        score = jnp.einsum("bqh,bqhs->bqs", wb.astype(jnp.float32) * (HI ** -0.5), jax.nn.relu(dots))
        score = jnp.where(causal[None], score, -jnp.inf)
        _, sel = lax.top_k(score, top)
        valid = sel <= qpos[None, :, None]
        k_sel = gather(k, sel)
        v_sel = gather(v, sel)
        att = jnp.einsum("bqhd,bqkhd->bhqk", qb, k_sel).astype(jnp.float32) * (Dh ** -0.5)
        att = jnp.where(valid[:, None], att, -jnp.inf)
        prob = jax.nn.softmax(att, axis=-1).astype(v.dtype)
        return jnp.einsum("bhqk,bqkhd->bqhd", prob, v_sel)

    starts = jnp.arange(n_blk) * Q_BLOCK
    out = lax.map(one_block, (to_blocks(q), to_blocks(q_idx), to_blocks(w_idx), starts))
    return jnp.moveaxis(out, 0, 1).reshape(B, S, H, Dh)


def token_mixer(u, w_in, conv_w, a_log, dt_bias, dn_norm_g, idx_k_norm_g,
                w_br_a, w_br_b, w_out, cos, sin):
    B, S, _ = u.shape
    offsets = np.cumsum(np.array(MIX_SIZES))[:-1].tolist()
    (q_a, k_a, v_a, z_a, a_a, b_a, q_b, k_b, v_b, q_i, k_i, w_i, g_a, g_b) = jnp.split(
        u @ w_in, offsets, axis=-1)

    qkv = jax.nn.silu(causal_conv(jnp.concatenate([q_a, k_a, v_a], axis=-1), conv_w))
    q_a, k_a, v_a = jnp.split(qkv, 3, axis=-1)
    qa = l2norm(q_a.reshape(B, S, H_A, DK_A)) * (DK_A ** -0.5)
    ka = l2norm(k_a.reshape(B, S, H_A, DK_A))
    va = v_a.reshape(B, S, H_A, DV_A).astype(jnp.float32)
    g = -jnp.exp(a_log.astype(jnp.float32)) * jax.nn.softplus(
        a_a.astype(jnp.float32) + dt_bias.astype(jnp.float32))
    beta = jax.nn.sigmoid(b_a.astype(jnp.float32))
    o_a = gated_delta_rule(qa, ka, va, g, beta).astype(u.dtype)
    o_a = rmsnorm(o_a, dn_norm_g) * jax.nn.silu(z_a.reshape(B, S, H_A, DV_A))
    y_a = o_a.reshape(B, S, W_A) @ w_br_a

    qb = partial_rope(q_b.reshape(B, S, H_B, DH_B), cos, sin)
    kb = partial_rope(k_b.reshape(B, S, H_B, DH_B), cos, sin)
    vb = v_b.reshape(B, S, H_B, DH_B)
    qi = partial_rope(q_i.reshape(B, S, HI, DI), cos, sin)
    ki = partial_rope(rmsnorm(k_i, idx_k_norm_g)[:, :, None, :], cos, sin)[:, :, 0, :]
    o_b = dsa_attention(qb, kb, vb, qi, ki, w_i)
    y_b = o_b.reshape(B, S, W_B) @ w_br_b

    merged = jax.nn.sigmoid(g_a) * y_a + jax.nn.sigmoid(g_b) * y_b
    return merged @ w_out


def setup_inputs(seed: int = 0) -> dict:
    key = jax.random.key(seed)
    ks = iter(jax.random.split(key, 40))
    f32 = jnp.float32
    L, D = DEPTH, D_MODEL

    def nrm(shape, fan_in):
        return jax.random.normal(next(ks), shape, f32) * (fan_in ** -0.5)

    def gain(shape):
        return 1.0 + 0.02 * jax.random.normal(next(ks), shape, f32)

    x = jax.random.normal(next(ks), (BATCH, SEQ, D), f32)
    p = jax.random.normal(next(ks), (DEPTH, BATCH, SEQ, PLE_DIM), f32)
    a_log = jnp.log(jax.random.uniform(next(ks), (L, H_A), f32, 1.0, 16.0))
    dt = jnp.exp(jax.random.uniform(next(ks), (L, H_A), f32, math.log(0.001), math.log(0.1)))
    dt_bias = dt + jnp.log(-jnp.expm1(-dt))
    return {
        "x": x,
        "p": p,
        "ffn1_norm_pre": gain((L, D)),
        "ffn1_norm_post": gain((L, D)),
        "ffn1_w_in": nrm((L, D, 2 * D_FF), D),
        "ffn1_w_out": nrm((L, D_FF, D), D_FF),
        "mix_norm_pre": gain((L, D)),
        "mix_norm_post": gain((L, D)),
        "mix_w_in": nrm((L, D, MIX_COLS), D),
        "conv_w": nrm((L, CONV_K, 3 * W_A), CONV_K),
        "a_log": a_log,
        "dt_bias": dt_bias,
        "dn_norm_g": gain((L, DV_A)),
        "idx_k_norm_g": gain((L, DI)),
        "w_br_a": nrm((L, W_A, D), W_A),
        "w_br_b": nrm((L, W_B, D), W_B),
        "mix_w_out": nrm((L, D, D), D),
        "ffn2_norm_pre": gain((L, D)),
        "ffn2_norm_post": gain((L, D)),
        "ffn2_w_in": nrm((L, D, 2 * D_FF), D),
        "ffn2_w_out": nrm((L, D_FF, D), D_FF),
        "ple_norm_pre": gain((L, D)),
        "ple_norm_post": gain((L, D)),
        "ple_w_gate": nrm((L, D, D), D),
        "ple_w_proj": nrm((L, PLE_DIM, D), PLE_DIM),
    }


def reference(x, p, ffn1_norm_pre, ffn1_norm_post, ffn1_w_in, ffn1_w_out,
              mix_norm_pre, mix_norm_post, mix_w_in, conv_w, a_log, dt_bias,
              dn_norm_g, idx_k_norm_g, w_br_a, w_br_b, mix_w_out,
              ffn2_norm_pre, ffn2_norm_post, ffn2_w_in, ffn2_w_out,
              ple_norm_pre, ple_norm_post, ple_w_gate, ple_w_proj):
    cos, sin = rope_tables(x.shape[1])
    h = x
    for i in range(DEPTH):
        h = h + 0.5 * rmsnorm(swiglu(rmsnorm(h, ffn1_norm_pre[i]), ffn1_w_in[i], ffn1_w_out[i]),
                              ffn1_norm_post[i])
        mix = token_mixer(rmsnorm(h, mix_norm_pre[i]), mix_w_in[i], conv_w[i], a_log[i], dt_bias[i],
                          dn_norm_g[i], idx_k_norm_g[i], w_br_a[i], w_br_b[i], mix_w_out[i], cos, sin)
        h = h + rmsnorm(mix, mix_norm_post[i])
        h = h + 0.5 * rmsnorm(swiglu(rmsnorm(h, ffn2_norm_pre[i]), ffn2_w_in[i], ffn2_w_out[i]),
                              ffn2_norm_post[i])
        gate = jax.nn.sigmoid(rmsnorm(h, ple_norm_pre[i]) @ ple_w_gate[i])
        h = h + rmsnorm(gate * (p[i] @ ple_w_proj[i]), ple_norm_post[i])
    return h
```

```python
import functools
import math

import jax
import jax.numpy as jnp
from jax import lax
from jax.experimental import pallas as pl
from jax.experimental.pallas import tpu as pltpu

F32 = jnp.float32
BF16 = jnp.bfloat16
I32 = jnp.int32

EPS = 1e-6
N_HEADS = 8
HEAD_DIM = 64
GROUP_W = N_HEADS * HEAD_DIM
CONV_TAPS = 4
ROT_DIM = HEAD_DIM // 4
ROPE_THETA = 500000.0
TOPK_MAX = 256
LANES = 128
SUBLANES = 8
FLT_MIN_NORMAL = 1.1754943508222875e-38
LOG2_E = 1.4426950408889634
SEARCH_FAST_PASSES = 24
SEARCH_MAX_PASSES = 256
VMEM_LIMIT = 56 * 1024 * 1024

SM_KI = 0
SM_WI = 64
SM_A = 72
SM_B = 80


def _rms(x):
    return x * lax.rsqrt(jnp.mean(x * x, axis=-1, keepdims=True) + EPS)


def _dot(a, b):
    return jnp.dot(a, b, preferred_element_type=F32)


def _dot_nt(a, b):
    return lax.dot_general(a, b, (((1,), (1,)), ((), ())), preferred_element_type=F32)


def _split2(x):
    hi = x.astype(BF16)
    lo = (x - hi.astype(F32)).astype(BF16)
    return hi, lo


def _split3(x):
    hi = x.astype(BF16)
    r = x - hi.astype(F32)
    mid = r.astype(BF16)
    lo = (r - mid.astype(F32)).astype(BF16)
    return hi, mid, lo


def _softplus(x):
    return jnp.maximum(x, 0.0) + jnp.log(1.0 + jnp.exp(-jnp.abs(x)))


def _sigmoid(x):
    return 1.0 / (1.0 + jnp.exp(-x))


def _ffn_kernel(x_ref, gpre_ref, wg_ref, wu_ref, wo_ref, gpost_ref, o_ref, xn_ref, acc_ref, *, nj):
    j = pl.program_id(1)

    @pl.when(j == 0)
    def _():
        xn_ref[...] = (_rms(x_ref[...]) * gpre_ref[...]).astype(BF16)
        acc_ref[...] = jnp.zeros_like(acc_ref)

    xn = xn_ref[...]
    g = _dot(xn, wg_ref[...])
    u = _dot(xn, wu_ref[...])
    a = (g * _sigmoid(g) * u).astype(BF16)
    acc_ref[...] += _dot(a, wo_ref[...])

    @pl.when(j == nj - 1)
    def _():
        o_ref[...] = x_ref[...] + 0.5 * (_rms(acc_ref[...]) * gpost_ref[...])


def _ffn(h, gpre, w_in, w_out, gpost, *, tm, tf):
    T, D = h.shape
    FF = w_out.shape[0]
    nj = FF // tf
    return pl.pallas_call(
        functools.partial(_ffn_kernel, nj=nj),
        grid=(T // tm, nj),
        in_specs=[
            pl.BlockSpec((tm, D), lambda i, j: (i, 0)),
            pl.BlockSpec((1, D), lambda i, j: (0, 0)),
            pl.BlockSpec((D, tf), lambda i, j: (0, j)),
            pl.BlockSpec((D, tf), lambda i, j: (0, j + nj)),
            pl.BlockSpec((tf, D), lambda i, j: (j, 0)),
            pl.BlockSpec((1, D), lambda i, j: (0, 0)),
        ],
        out_specs=pl.BlockSpec((tm, D), lambda i, j: (i, 0)),
        out_shape=jax.ShapeDtypeStruct((T, D), F32),
        scratch_shapes=[pltpu.VMEM((tm, D), BF16), pltpu.VMEM((tm, D), F32)],
        compiler_params=pltpu.CompilerParams(
            dimension_semantics=("parallel", "arbitrary"), vmem_limit_bytes=VMEM_LIMIT),
        name="ffn",
    )(h, gpre, w_in, w_in, w_out, gpost)


def _rope(x, c, sa, sb):
    half = ROT_DIM // 2
    return x * c + pltpu.roll(x, LANES - half, 1) * sa + pltpu.roll(x, half, 1) * sb


def _mix_in_kernel(h_ref, gpre_ref, w_ref, c_ref, sa_ref, sb_ref, kig_ref, smscale_ref,
                   qkva_ref, z_ref, qb_ref, kb_ref, vb_ref, qi_ref, g_ref, small_ref):
    xn = (_rms(h_ref[...]) * gpre_ref[...]).astype(BF16)
    c, sa, sb = c_ref[...], sa_ref[...], sb_ref[...]
    W = GROUP_W

    def proj(lo, hi):
        return _dot(xn, w_ref[:, lo:hi])

    qkva_ref[...] = proj(0, 3 * W)
    z_ref[...] = proj(3 * W, 4 * W)

    def roped(lo, scale):
        y = proj(lo, lo + W)
        parts = [_rope(y[:, t * LANES:(t + 1) * LANES], c, sa, sb) for t in range(W // LANES)]
        return (jnp.concatenate(parts, axis=1) * scale).astype(BF16)

    inv_sqrt_d = HEAD_DIM ** -0.5
    qb_ref[...] = roped(4 * W, inv_sqrt_d * LOG2_E)
    kb_ref[...] = roped(5 * W, 1.0)
    vb_ref[...] = proj(6 * W, 7 * W).astype(BF16)
    qi_ref[...] = roped(7 * W, inv_sqrt_d)
    g_ref[...] = proj(8 * W, 12 * W)

    sm = proj(12 * W, 12 * W + LANES)
    lane = lax.broadcasted_iota(I32, sm.shape, 1)
    is_ki = lane < HEAD_DIM
    ms = jnp.sum(jnp.where(is_ki, sm * sm, 0.0), axis=-1, keepdims=True) * (1.0 / HEAD_DIM)
    ki = _rope(sm * lax.rsqrt(ms + EPS) * kig_ref[...], c, sa, sb)
    small_ref[...] = jnp.where(is_ki, ki, sm * smscale_ref[...])


def _mix_in(h, gpre, w, rope_c, rope_sa, rope_sb, kig, smscale, *, tm, seq):
    T, D = h.shape
    W = GROUP_W
    ncol = w.shape[1]
    nseq = seq // tm
    row = lambda i: (i, 0)
    const = lambda i: (0, 0)
    pos = lambda i: (i % nseq, 0)
    outs = [(3 * W, F32), (W, F32), (W, BF16), (W, BF16), (W, BF16), (W, BF16), (4 * W, F32), (LANES, F32)]
    return pl.pallas_call(
        _mix_in_kernel,
        grid=(T // tm,),
        in_specs=[
            pl.BlockSpec((tm, D), row),
            pl.BlockSpec((1, D), const),
            pl.BlockSpec((D, ncol), const),
            pl.BlockSpec((tm, LANES), pos),
            pl.BlockSpec((tm, LANES), pos),
            pl.BlockSpec((tm, LANES), pos),
            pl.BlockSpec((1, LANES), const),
            pl.BlockSpec((1, LANES), const),
        ],
        out_specs=[pl.BlockSpec((tm, n), row) for n, _ in outs],
        out_shape=[jax.ShapeDtypeStruct((T, n), dt) for n, dt in outs],
        compiler_params=pltpu.CompilerParams(
            dimension_semantics=("parallel",), vmem_limit_bytes=VMEM_LIMIT),
        name="mix_in",
    )(h, gpre, w, rope_c, rope_sa, rope_sb, kig, smscale)


def _gdn_kernel(qkv_ref, z_ref, small_ref, convw_ref, alog_ref, dtb_ref, alogc_ref, dtbc_ref,
                dng_ref, bd_ref, o_ref,
                xs_ref, q3_ref, k3_ref, v3_ref, t_ref, pw_ref, qkd_ref, rhs_ref, u_ref, wd_ref,
                qd_ref, kdT_ref, oc_ref, state_ref, *, P, C):
    W = GROUP_W
    HD = HEAD_DIM
    H = N_HEADS
    shift = int(math.log2(C))
    i = pl.program_id(1)

    @pl.when(i == 0)
    def _():
        xs_ref[0:8, :] = jnp.zeros((8, 3 * W), F32)
        state_ref[...] = jnp.zeros_like(state_ref)

    xs_ref[8:8 + P, :] = qkv_ref[...]
    w = convw_ref[...]
    y = w[CONV_TAPS - 1:CONV_TAPS, :] * xs_ref[8:8 + P, :]
    for tap in range(CONV_TAPS - 1):
        off = 8 - (CONV_TAPS - 1) + tap
        y = y + w[tap:tap + 1, :] * xs_ref[off:off + P, :]
    xs_ref[0:8, :] = xs_ref[P:P + 8, :]
    y = y * _sigmoid(y)
    q, k, v = y[:, :W], y[:, W:2 * W], y[:, 2 * W:]

    bd = bd_ref[...]

    def head_sum(x):
        hi, lo = _split2(x)
        return _dot(hi, bd) + _dot(lo, bd)

    qn = q * lax.rsqrt(head_sum(q * q) + EPS) * (HD ** -0.5)
    kn = k * lax.rsqrt(head_sum(k * k) + EPS)
    for h in range(H):
        sl = slice(h * HD, (h + 1) * HD)
        q3_ref[h] = qn[:, sl]
        k3_ref[h] = kn[:, sl]
        v3_ref[h] = v[:, sl]

    sm = small_ref[...]
    gfull = -jnp.exp(alog_ref[...]) * _softplus(sm + dtb_ref[...])
    bfull = _sigmoid(sm)
    gT = -jnp.exp(alogc_ref[...]) * _softplus(sm.T[SM_A:SM_A + H, :] + dtbc_ref[...])

    ri = lax.broadcasted_iota(I32, (P, P), 0)
    ci = lax.broadcasted_iota(I32, (P, P), 1)
    same = lax.shift_right_logical(ri, shift) == lax.shift_right_logical(ci, shift)
    incl = jnp.logical_and(same, ri >= ci)
    diag = ri == ci
    tri = jnp.where(incl, 1.0, 0.0).astype(BF16)
    triT = jnp.where(same, jnp.where(ci >= ri, 1.0, 0.0), 0.0).astype(BF16)
    blk = jnp.where(same, 1.0, 0.0).astype(BF16)

    g_parts = _split3(gfull)
    gc = sum(_dot(tri, part) for part in g_parts)
    gtot = sum(_dot(blk, part) for part in g_parts)
    gcT = sum(_dot(part, triT) for part in _split3(gT))
    egc = jnp.exp(gc)
    ekd = jnp.exp(gtot - gc)

    for h in range(H):
        K, Q, V = k3_ref[h], q3_ref[h], v3_ref[h]
        Kb = K.astype(BF16)
        gcol = gc[:, SM_A + h:SM_A + h + 1]
        grow = gcT[h:h + 1, :]
        bcol = bfull[:, SM_B + h:SM_B + h + 1]
        ecol = egc[:, SM_A + h:SM_A + h + 1]
        decay = jnp.where(incl, jnp.exp(gcol - grow), 0.0)
        n_mat = jnp.where(diag, 0.0, -(bcol * _dot_nt(Kb, Kb) * decay))
        t_ref[h] = jnp.where(diag, 1.0, n_mat)
        pw_ref[0, h] = n_mat.astype(BF16)
        qkd_ref[h] = (_dot_nt(Q.astype(BF16), Kb) * decay).astype(BF16)
        rhs_ref[h] = jnp.concatenate([V * bcol, K * (bcol * ecol)], axis=1).astype(BF16)
        qd_ref[h] = (Q * ecol).astype(BF16)
        kdT_ref[h] = (K * ekd[:, SM_A + h:SM_A + h + 1]).T.astype(BF16)

    for it in range(shift - 1):
        src, dst = it % 2, 1 - it % 2
        for h in range(H):
            pw = pw_ref[src, h]
            pw_ref[dst, h] = _dot(pw, pw).astype(BF16)
        for h in range(H):
            t = t_ref[h]
            t_ref[h] = t + _dot(t.astype(BF16), pw_ref[dst, h])
    for h in range(H):
        sol = _dot(t_ref[h].astype(BF16), rhs_ref[h])
        u_ref[h] = sol[:, :HD]
        wd_ref[h] = sol[:, HD:].astype(BF16)

    for c in range(P // C):
        rows = slice(c * C, (c + 1) * C)
        states = [state_ref[h] for h in range(H)]
        sbf = [s.astype(BF16) for s in states]
        v_new = [(u_ref[h, rows, :] - _dot(wd_ref[h, rows, :], sbf[h])).astype(BF16) for h in range(H)]
        for h in range(H):
            oc_ref[rows, h * HD:(h + 1) * HD] = (
                _dot(qd_ref[h, rows, :], sbf[h]) + _dot(qkd_ref[h, rows, rows], v_new[h]))
        for h in range(H):
            g_tot = jnp.exp(gcT[h:h + 1, (c + 1) * C - 1:(c + 1) * C])
            state_ref[h] = states[h] * g_tot + _dot(kdT_ref[h, :, rows], v_new[h])

    o = oc_ref[...]
    ms = head_sum(o * o) * (1.0 / HD)
    zz = z_ref[...]
    o_ref[...] = (o * lax.rsqrt(ms + EPS) * dng_ref[...] * (zz * _sigmoid(zz))).astype(BF16)


def _gdn(qkva, z, small, convw, alog_row, dtb_row, alog_col, dtb_col, dng, bd, *, batch, seq, P, C):
    T = qkva.shape[0]
    W = GROUP_W
    H, HD = N_HEADS, HEAD_DIM
    nblk = seq // P
    row = lambda b, i: (b * nblk + i, 0)
    const = lambda b, i: (0, 0)
    return pl.pallas_call(
        functools.partial(_gdn_kernel, P=P, C=C),
        grid=(batch, nblk),
        in_specs=[
            pl.BlockSpec((P, 3 * W), row),
            pl.BlockSpec((P, W), row),
            pl.BlockSpec((P, LANES), row),
            pl.BlockSpec((CONV_TAPS, 3 * W), const),
            pl.BlockSpec((1, LANES), const),
            pl.BlockSpec((1, LANES), const),
            pl.BlockSpec((H, 1), const),
            pl.BlockSpec((H, 1), const),
            pl.BlockSpec((1, W), const),
            pl.BlockSpec((W, W), const),
        ],
        out_specs=pl.BlockSpec((P, W), row),
        out_shape=jax.ShapeDtypeStruct((T, W), BF16),
        scratch_shapes=[
            pltpu.VMEM((P + 8, 3 * W), F32),
            pltpu.VMEM((H, P, HD), F32),
            pltpu.VMEM((H, P, HD), F32),
            pltpu.VMEM((H, P, HD), F32),
            pltpu.VMEM((H, P, P), F32),
            pltpu.VMEM((2, H, P, P), BF16),
            pltpu.VMEM((H, P, P), BF16),
            pltpu.VMEM((H, P, 2 * HD), BF16),
            pltpu.VMEM((H, P, HD), F32),
            pltpu.VMEM((H, P, HD), BF16),
            pltpu.VMEM((H, P, HD), BF16),
            pltpu.VMEM((H, HD, P), BF16),
            pltpu.VMEM((P, W), F32),
            pltpu.VMEM((H, HD, HD), F32),
        ],
        compiler_params=pltpu.CompilerParams(
            dimension_semantics=("parallel", "arbitrary"), vmem_limit_bytes=VMEM_LIMIT),
        name="gdn",
    )(qkva, z, small, convw, alog_row, dtb_row, alog_col, dtb_col, dng, bd)


def _key_to_f32(key):
    bits = key ^ (lax.shift_right_arithmetic(key, 31) & 0x7FFFFFFF)
    return pltpu.bitcast(bits, F32)


def _f32_to_key(x):
    bits = pltpu.bitcast(x, I32)
    return bits ^ (lax.shift_right_arithmetic(bits, 31) & 0x7FFFFFFF)


def _dsa_kernel(qb_ref, qi_ref, wsm_ref, kb_ref, vb_ref, ksm_ref, o_ref,
                sc_ref, vT_ref, kib_ref, s_ref, acc_ref, *, Tq, Tk, seq, topk):
    HD = HEAD_DIM
    H = N_HEADS
    ACC = 4 * SUBLANES
    i = pl.program_id(1)
    nkt = ((i + 1) * Tq + Tk - 1) // Tk
    qpos = i * Tq + lax.broadcasted_iota(I32, (1, Tq), 1)
    sub_pos = lax.broadcasted_iota(I32, (Tk, Tq), 0)

    @pl.when(i == 0)
    def _():
        for kt in range(seq // Tk):
            vT_ref[kt] = vb_ref[kt * Tk:(kt + 1) * Tk, :].astype(F32).T.astype(BF16)
        kib_ref[...] = ksm_ref[...].astype(BF16)

    wT = wsm_ref[...].T
    qi_heads = [qi_ref[:, h * HD:(h + 1) * HD] for h in range(H)]
    w_rows = [wT[SM_WI + h:SM_WI + h + 1, :] for h in range(H)]

    def fold(m):
        return jnp.sum(m.reshape(Tk // ACC, ACC, Tq), axis=0)

    def score_body(kt, carry):
        rmax, rmin, ge0, gt0 = carry
        r0 = pl.multiple_of(kt * Tk, Tk)
        ki = kib_ref[pl.ds(r0, Tk), :][:, SM_KI:SM_KI + HD]
        dots = [_dot_nt(ki, qi_heads[h]) for h in range(H)]
        score = jnp.zeros((Tk, Tq), F32)
        for h in range(H):
            score = score + w_rows[h] * jnp.maximum(dots[h], 0.0)
        adm = r0 + sub_pos <= qpos
        sc = jnp.where(adm, score, -jnp.inf)
        sc_ref[pl.ds(r0, Tk), :] = sc
        rmax = jnp.maximum(rmax, jnp.max(sc, axis=0, keepdims=True))
        rmin = jnp.minimum(rmin, jnp.min(jnp.where(adm, score, jnp.inf), axis=0, keepdims=True))
        ge0 = ge0 + fold(jnp.where(sc >= 0.0, 1.0, 0.0))
        gt0 = gt0 + fold(jnp.where(sc > 0.0, 1.0, 0.0))
        return rmax, rmin, ge0, gt0

    row = lambda v: jnp.full((1, Tq), v, F32)
    rmax, rmin, ge0, gt0 = lax.fori_loop(
        0, nkt, score_body, (row(-jnp.inf), row(jnp.inf), jnp.zeros((ACC, Tq), F32), jnp.zeros((ACC, Tq), F32)))
    cnt_ge0 = jnp.sum(ge0, axis=0, keepdims=True)
    cnt_gt0 = jnp.sum(gt0, axis=0, keepdims=True)

    def count(pred):
        def body(kt, acc):
            r0 = pl.multiple_of(kt * Tk, Tk)
            return acc + fold(pred(r0, sc_ref[pl.ds(r0, Tk), :]))
        acc = lax.fori_loop(0, nkt, body, jnp.zeros((ACC, Tq), F32))
        return jnp.sum(acc, axis=0, keepdims=True)

    def count_ge(v):
        return count(lambda r0, s: jnp.where(s >= v, 1.0, 0.0))

    def count_gt(v):
        return count(lambda r0, s: jnp.where(s > v, 1.0, 0.0))

    kf = float(topk)
    log_k = math.log(kf)
    short = qpos < topk
    zero_tie = jnp.logical_and(cnt_gt0 < kf, cnt_ge0 >= kf)
    positive = cnt_gt0 >= kf
    tiny = row(FLT_MIN_NORMAL)
    lo_k = jnp.where(positive, _f32_to_key(tiny), _f32_to_key(rmin))
    hi_k = jnp.where(positive, _f32_to_key(rmax) + 1, _f32_to_key(-tiny))
    c_lo = jnp.where(positive, cnt_gt0, (qpos + 1).astype(F32))
    c_hi = jnp.where(positive, 0.0, cnt_ge0)
    done0 = jnp.where(jnp.logical_or(jnp.logical_or(short, zero_tie), c_lo == kf), 1.0, 0.0)

    def probe(st):
        it, lo_k, hi_k, c_lo, c_hi, w_lo, w_hi, last, done = st
        lo = _key_to_f32(lo_k)
        hi = _key_to_f32(hi_k - 1)
        a = (jnp.log(c_lo + 0.5) - log_k) * w_lo
        b = (log_k - jnp.log(c_hi + 0.5)) * w_hi
        frac = jnp.clip(a / jnp.maximum(a + b, 1e-30), 0.02, 0.98)
        cand_k = _f32_to_key(lo + frac * (hi - lo))
        mid_k = lo_k + lax.shift_right_logical(hi_k - lo_k, 1)
        cand_k = jnp.where(jnp.logical_and(it >= SEARCH_FAST_PASSES, it % 4 == 3), mid_k, cand_k)
        cand_k = jnp.minimum(jnp.maximum(cand_k, lo_k + 1), hi_k - 1)
        c = count_ge(_key_to_f32(cand_k))
        live = done == 0.0
        up = jnp.logical_and(live, c >= kf)
        dn = jnp.logical_and(live, c < kf)
        w_hi = jnp.where(jnp.logical_and(up, last > 0.0), w_hi * 0.5, jnp.where(dn, 1.0, w_hi))
        w_lo = jnp.where(jnp.logical_and(dn, last < 0.0), w_lo * 0.5, jnp.where(up, 1.0, w_lo))
        last = jnp.where(up, 1.0, jnp.where(dn, -1.0, last))
        lo_k = jnp.where(up, cand_k, lo_k)
        c_lo = jnp.where(up, c, c_lo)
        hi_k = jnp.where(dn, cand_k, hi_k)
        c_hi = jnp.where(dn, c, c_hi)
        done = jnp.where(jnp.logical_or(c_lo == kf, hi_k - lo_k <= 1), 1.0, done)
        return it + 1, lo_k, hi_k, c_lo, c_hi, w_lo, w_hi, last, done

    def search_cond(st):
        return jnp.logical_and(st[0] < SEARCH_MAX_PASSES, jnp.min(st[-1]) == 0.0)

    st = (jnp.int32(0), lo_k, hi_k, c_lo, c_hi, row(1.0), row(1.0), row(0.0), done0)
    st = lax.while_loop(search_cond, lambda st: probe(probe(st)), st)
    _, lo_k, hi_k, c_lo, c_hi = st[:5]

    lo_f = jnp.where(short, -jnp.inf, jnp.where(zero_tie, 0.0, _key_to_f32(lo_k)))
    hi_f = jnp.where(zero_tie, tiny, _key_to_f32(hi_k))
    c_lo = jnp.where(zero_tie, cnt_ge0, c_lo)
    c_hi = jnp.where(zero_tie, cnt_gt0, c_hi)
    cnt_gt = count_gt(lo_f)

    def unresolved(lo_f, hi_f, c_lo, c_hi, cnt_gt):
        inside = c_lo - c_hi
        u = jnp.logical_and(kf - c_hi < inside, c_lo - cnt_gt < inside)
        return jnp.where(short, 0.0, jnp.where(u, 1.0, 0.0))

    def refine_cond(st):
        return jnp.logical_and(st[0] < SEARCH_MAX_PASSES, jnp.max(unresolved(*st[1:])) > 0.0)

    def refine(st):
        it, lo_f, hi_f, c_lo, c_hi, cnt_gt = st
        live = unresolved(lo_f, hi_f, c_lo, c_hi, cnt_gt) > 0.0
        mid = 0.5 * lo_f + 0.5 * hi_f
        c = count_ge(mid)
        up = jnp.logical_and(live, c >= kf)
        dn = jnp.logical_and(live, c < kf)
        lo_f = jnp.where(up, mid, lo_f)
        c_lo = jnp.where(up, c, c_lo)
        hi_f = jnp.where(dn, mid, hi_f)
        c_hi = jnp.where(dn, c, c_hi)
        return it + 1, lo_f, hi_f, c_lo, c_hi, count_gt(lo_f)

    _, thr, _, c_thr, _, cnt_gt = lax.while_loop(
        refine_cond, refine, (jnp.int32(0), lo_f, hi_f, c_lo, c_hi, cnt_gt))

    need = kf - cnt_gt
    cut = jnp.where(short, 0.0, jnp.where(c_thr - cnt_gt > need, 1.0, 0.0))
    any_cut = jnp.max(cut) > 0.0
    nbits = max(1, int(math.ceil(math.log2(seq))))

    def idx_body(t, p):
        cand = p | lax.shift_left(jnp.int32(1), nbits - 1 - t)

        def pred(r0, s):
            return jnp.where(s == thr, jnp.where(r0 + sub_pos < cand, 1.0, 0.0), 0.0)
        return jnp.where(count(pred) < need, cand, p)

    p_init = jnp.where(any_cut, 0, 2 ** nbits - 1) + jnp.zeros((1, Tq), I32)
    p_cut = lax.fori_loop(0, jnp.where(any_cut, nbits, 0), idx_body, p_init)
    p_cut = jnp.where(short, -1, p_cut)

    def mask_body(kt, carry):
        r0 = pl.multiple_of(kt * Tk, Tk)
        sc = sc_ref[pl.ds(r0, Tk), :]
        tie_ok = jnp.where(r0 + sub_pos <= p_cut, 0.0, -jnp.inf)
        sc_ref[pl.ds(r0, Tk), :] = jnp.where(sc > thr, 0.0, jnp.where(sc == thr, tie_ok, -jnp.inf))
        return carry

    lax.fori_loop(0, nkt, mask_body, 0)

    acc_ref[...] = jnp.zeros_like(acc_ref)

    def att_body(kt, carry):
        m_all, l_all = carry
        r0 = pl.multiple_of(kt * Tk, Tk)
        bias = sc_ref[pl.ds(r0, Tk), :]
        tile_max = []
        for h in range(H):
            sl = slice(h * HD, (h + 1) * HD)
            s = _dot_nt(kb_ref[pl.ds(r0, Tk), sl], qb_ref[:, sl]) + bias
            s_ref[h] = s
            tile_max.append(jnp.max(s, axis=0, keepdims=True))
        m_rows, l_rows = [], []
        for h in range(H):
            s = s_ref[h]
            m_old = m_all[h:h + 1, :]
            m_new = jnp.maximum(m_old, tile_max[h])
            alpha = jnp.exp2(m_old - m_new)
            p = jnp.exp2(s - m_new)
            l_rows.append(alpha * l_all[h:h + 1, :] + jnp.sum(p, axis=0, keepdims=True))
            acc_ref[h] = alpha * acc_ref[h] + _dot(vT_ref[kt, h * HD:(h + 1) * HD, :], p.astype(BF16))
            m_rows.append(m_new)
        return jnp.concatenate(m_rows, axis=0), jnp.concatenate(l_rows, axis=0)

    init = (jnp.full((H, Tq), -1e30, F32), jnp.zeros((H, Tq), F32))
    _, l_fin = lax.fori_loop(0, nkt, att_body, init)
    out_t = jnp.concatenate([acc_ref[h] / l_fin[h:h + 1, :] for h in range(H)], axis=0)
    o_ref[...] = out_t.T.astype(BF16)


def _dsa(qb, kb, vb, qi, small, *, batch, seq, Tq, Tk):
    T = qb.shape[0]
    W = GROUP_W
    nq = seq // Tq
    qrow = lambda b, i: (b * nq + i, 0)
    full = lambda b, i: (b, 0)
    topk = min(TOPK_MAX, seq // 4)
    return pl.pallas_call(
        functools.partial(_dsa_kernel, Tq=Tq, Tk=Tk, seq=seq, topk=topk),
        grid=(batch, nq),
        in_specs=[
            pl.BlockSpec((Tq, W), qrow),
            pl.BlockSpec((Tq, W), qrow),
            pl.BlockSpec((Tq, LANES), qrow),
            pl.BlockSpec((seq, W), full),
            pl.BlockSpec((seq, W), full),
            pl.BlockSpec((seq, LANES), full),
        ],
        out_specs=pl.BlockSpec((Tq, W), qrow),
        out_shape=jax.ShapeDtypeStruct((T, W), BF16),
        scratch_shapes=[
            pltpu.VMEM((seq, Tq), F32),
            pltpu.VMEM((seq // Tk, W, Tk), BF16),
            pltpu.VMEM((seq, LANES), BF16),
            pltpu.VMEM((N_HEADS, Tk, Tq), F32),
            pltpu.VMEM((N_HEADS, HEAD_DIM, Tq), F32),
        ],
        compiler_params=pltpu.CompilerParams(
            dimension_semantics=("parallel", "arbitrary"), vmem_limit_bytes=VMEM_LIMIT),
        name="dsa",
    )(qb, qi, small, kb, vb, small)


def _merge_kernel(h_ref, oa_ref, ob_ref, g_ref, wa_ref, wb_ref, wo_ref, gpost_ref, o_ref):
    D = h_ref.shape[1]
    ya = _dot(oa_ref[...], wa_ref[...])
    yb = _dot(ob_ref[...], wb_ref[...])
    g = g_ref[...]
    merged = _sigmoid(g[:, :D]) * ya + _sigmoid(g[:, D:]) * yb
    mix = _dot(merged.astype(BF16), wo_ref[...])
    o_ref[...] = h_ref[...] + _rms(mix) * gpost_ref[...]


def _merge(h, oa, ob, g, wa, wb, wo, gpost, *, tm):
    T, D = h.shape
    W = GROUP_W
    row = lambda i: (i, 0)
    const = lambda i: (0, 0)
    return pl.pallas_call(
        _merge_kernel,
        grid=(T // tm,),
        in_specs=[
            pl.BlockSpec((tm, D), row),
            pl.BlockSpec((tm, W), row),
            pl.BlockSpec((tm, W), row),
            pl.BlockSpec((tm, 2 * D), row),
            pl.BlockSpec((W, D), const),
            pl.BlockSpec((W, D), const),
            pl.BlockSpec((D, D), const),
            pl.BlockSpec((1, D), const),
        ],
        out_specs=pl.BlockSpec((tm, D), row),
        out_shape=jax.ShapeDtypeStruct((T, D), F32),
        compiler_params=pltpu.CompilerParams(
            dimension_semantics=("parallel",), vmem_limit_bytes=VMEM_LIMIT),
        name="merge",
    )(h, oa, ob, g, wa, wb, wo, gpost)


def _ple_kernel(h_ref, p_ref, gpre_ref, wg_ref, wp_ref, gpost_ref, o_ref):
    h = h_ref[...]
    gate = _sigmoid(_dot((_rms(h) * gpre_ref[...]).astype(BF16), wg_ref[...]))
    e = _dot(p_ref[...].astype(BF16), wp_ref[...])
    o_ref[...] = h + _rms(gate * e) * gpost_ref[...]


def _ple(h, p, gpre, wg, wp, gpost, *, tm):
    T, D = h.shape
    P = p.shape[1]
    row = lambda i: (i, 0)
    const = lambda i: (0, 0)
    return pl.pallas_call(
        _ple_kernel,
        grid=(T // tm,),
        in_specs=[
            pl.BlockSpec((tm, D), row),
            pl.BlockSpec((tm, P), row),
            pl.BlockSpec((1, D), const),
            pl.BlockSpec((D, D), const),
            pl.BlockSpec((P, D), const),
            pl.BlockSpec((1, D), const),
        ],
        out_specs=pl.BlockSpec((tm, D), row),
        out_shape=jax.ShapeDtypeStruct((T, D), F32),
        compiler_params=pltpu.CompilerParams(
            dimension_semantics=("parallel",), vmem_limit_bytes=VMEM_LIMIT),
        name="ple",
    )(h, p, gpre, wg, wp, gpost)


def _rope_tables(seq):
    half = ROT_DIM // 2
    pos = jnp.arange(seq, dtype=F32)
    inv = ROPE_THETA ** (-jnp.arange(0, ROT_DIM, 2, dtype=F32) / ROT_DIM)
    ang = pos[:, None] * inv[None, :]
    cos, sin = jnp.cos(ang), jnp.sin(ang)
    ones = jnp.ones((seq, HEAD_DIM - ROT_DIM), F32)
    zeros_h = jnp.zeros((seq, half), F32)
    zeros_r = jnp.zeros((seq, HEAD_DIM - ROT_DIM), F32)
    c = jnp.concatenate([cos, cos, ones], axis=1)
    sa = jnp.concatenate([-sin, zeros_h, zeros_r], axis=1)
    sb = jnp.concatenate([zeros_h, sin, zeros_r], axis=1)
    rep = LANES // HEAD_DIM
    return jnp.tile(c, (1, rep)), jnp.tile(sa, (1, rep)), jnp.tile(sb, (1, rep))


def _pad_lanes(vec, offset):
    out = jnp.zeros((1, LANES), F32)
    return lax.dynamic_update_slice(out, vec.reshape(1, -1).astype(F32), (0, offset))


def kernel(x, p, ffn1_norm_pre, ffn1_norm_post, ffn1_w_in, ffn1_w_out, mix_norm_pre, mix_norm_post, mix_w_in, conv_w, a_log, dt_bias, dn_norm_g, idx_k_norm_g, w_br_a, w_br_b, mix_w_out, ffn2_norm_pre, ffn2_norm_post, ffn2_w_in, ffn2_w_out, ple_norm_pre, ple_norm_post, ple_w_gate, ple_w_proj):
    B, S, D = x.shape
    T = B * S
    W = GROUP_W
    depth = ffn1_w_in.shape[0]
    ff = ffn1_w_out.shape[1]

    tm_ffn = min(512, T)
    tf = ff // 2 if (ff // 2) % LANES == 0 else ff
    tm_mix = min(256, S)
    gdn_rows = min(256, S)
    gdn_chunk = min(64, gdn_rows)
    Tq = min(256, S)
    Tk = min(512, S)
    tm_post = min(512, T)

    rope_c, rope_sa, rope_sb = _rope_tables(S)
    ri = jnp.arange(W)
    bd = (ri[:, None] // HEAD_DIM == ri[None, :] // HEAD_DIM).astype(BF16)
    smscale = _pad_lanes(jnp.full((N_HEADS,), N_HEADS ** -0.5, F32), SM_WI)
    smscale = jnp.where(smscale == 0.0, 1.0, smscale)
    vec = lambda a: a.reshape(1, -1).astype(F32)

    h = x.reshape(T, D)
    for l in range(depth):
        h = _ffn(h, vec(ffn1_norm_pre[l]), ffn1_w_in[l].astype(BF16), ffn1_w_out[l].astype(BF16),
                 vec(ffn1_norm_post[l]), tm=tm_ffn, tf=tf)

        wi = mix_w_in[l]
        o_a, o_b = 4 * W, 4 * W + 2 * N_HEADS
        o_ki = o_b + 4 * W
        o_wi = o_ki + HEAD_DIM
        o_g = o_wi + N_HEADS
        w_cat = jnp.concatenate([
            wi[:, :4 * W], wi[:, o_b:o_ki], wi[:, o_g:o_g + 2 * D],
            wi[:, o_ki:o_g], wi[:, o_a:o_b],
            jnp.zeros((D, LANES - HEAD_DIM - 3 * N_HEADS), wi.dtype)], axis=1).astype(BF16)
        qkva, z, qb, kb, vb, qi, g, small = _mix_in(
            h, vec(mix_norm_pre[l]), w_cat, rope_c, rope_sa, rope_sb,
            _pad_lanes(idx_k_norm_g[l], SM_KI), smscale, tm=tm_mix, seq=S)

        o_gdn = _gdn(qkva, z, small, conv_w[l].astype(F32),
                     _pad_lanes(a_log[l], SM_A), _pad_lanes(dt_bias[l], SM_A),
                     a_log[l].reshape(N_HEADS, 1).astype(F32), dt_bias[l].reshape(N_HEADS, 1).astype(F32),
                     jnp.tile(vec(dn_norm_g[l]), (1, N_HEADS)), bd,
                     batch=B, seq=S, P=gdn_rows, C=gdn_chunk)
        o_dsa = _dsa(qb, kb, vb, qi, small, batch=B, seq=S, Tq=Tq, Tk=Tk)

        h = _merge(h, o_gdn, o_dsa, g, w_br_a[l].astype(BF16), w_br_b[l].astype(BF16),
                   mix_w_out[l].astype(BF16), vec(mix_norm_post[l]), tm=tm_post)
        h = _ffn(h, vec(ffn2_norm_pre[l]), ffn2_w_in[l].astype(BF16), ffn2_w_out[l].astype(BF16),
                 vec(ffn2_norm_post[l]), tm=tm_ffn, tf=tf)
        h = _ple(h, p[l].reshape(T, -1), vec(ple_norm_pre[l]), ple_w_gate[l].astype(BF16),
                 ple_w_proj[l].astype(BF16), vec(ple_norm_post[l]), tm=tm_post)
    return h.reshape(B, S, D)
```

```python
import functools
import math

import jax
import jax.numpy as jnp
from jax import lax
from jax.experimental import pallas as pl
from jax.experimental.pallas import tpu as pltpu

F32 = jnp.float32
BF16 = jnp.bfloat16
I32 = jnp.int32

EPS = 1e-6
N_HEADS = 8
HEAD_DIM = 64
GROUP_W = N_HEADS * HEAD_DIM
CONV_TAPS = 4
ROT_DIM = HEAD_DIM // 4
ROPE_THETA = 500000.0
TOPK_MAX = 256
LANES = 128
SUBLANES = 8
FLT_MIN_NORMAL = 1.1754943508222875e-38
LOG2_E = 1.4426950408889634
SEARCH_FAST_PASSES = 24
SEARCH_MAX_PASSES = 256
ATT_BOUND_MARGIN = 1.02
ATT_MIN_DENOM = 1e-30
VMEM_LIMIT = 56 * 1024 * 1024

SM_KI = 0
SM_WI = 64
SM_A = 72
SM_B = 80


def _rms(x):
    return x * lax.rsqrt(jnp.mean(x * x, axis=-1, keepdims=True) + EPS)


def _dot(a, b):
    return jnp.dot(a, b, preferred_element_type=F32)


def _dot_nt(a, b):
    return lax.dot_general(a, b, (((1,), (1,)), ((), ())), preferred_element_type=F32)


def _split2(x):
    hi = x.astype(BF16)
    lo = (x - hi.astype(F32)).astype(BF16)
    return hi, lo


def _split3(x):
    hi = x.astype(BF16)
    r = x - hi.astype(F32)
    mid = r.astype(BF16)
    lo = (r - mid.astype(F32)).astype(BF16)
    return hi, mid, lo


def _softplus(x):
    return jnp.maximum(x, 0.0) + jnp.log(1.0 + jnp.exp(-jnp.abs(x)))


def _sigmoid(x):
    return 1.0 / (1.0 + jnp.exp(-x))


def _ffn_kernel(x_ref, gpre_ref, wg_ref, wu_ref, wo_ref, gpost_ref, o_ref, xn_ref, acc_ref, *, nj):
    j = pl.program_id(1)

    @pl.when(j == 0)
    def _():
        xn_ref[...] = (_rms(x_ref[...]) * gpre_ref[...]).astype(BF16)
        acc_ref[...] = jnp.zeros_like(acc_ref)

    xn = xn_ref[...]
    g = _dot(xn, wg_ref[...])
    u = _dot(xn, wu_ref[...])
    a = (g * _sigmoid(g) * u).astype(BF16)
    acc_ref[...] += _dot(a, wo_ref[...])

    @pl.when(j == nj - 1)
    def _():
        o_ref[...] = x_ref[...] + 0.5 * (_rms(acc_ref[...]) * gpost_ref[...])


def _ffn(h, gpre, w_in, w_out, gpost, *, tm, tf):
    T, D = h.shape
    FF = w_out.shape[0]
    nj = FF // tf
    return pl.pallas_call(
        functools.partial(_ffn_kernel, nj=nj),
        grid=(T // tm, nj),
        in_specs=[
            pl.BlockSpec((tm, D), lambda i, j: (i, 0)),
            pl.BlockSpec((1, D), lambda i, j: (0, 0)),
            pl.BlockSpec((D, tf), lambda i, j: (0, j)),
            pl.BlockSpec((D, tf), lambda i, j: (0, j + nj)),
            pl.BlockSpec((tf, D), lambda i, j: (j, 0)),
            pl.BlockSpec((1, D), lambda i, j: (0, 0)),
        ],
        out_specs=pl.BlockSpec((tm, D), lambda i, j: (i, 0)),
        out_shape=jax.ShapeDtypeStruct((T, D), F32),
        scratch_shapes=[pltpu.VMEM((tm, D), BF16), pltpu.VMEM((tm, D), F32)],
        compiler_params=pltpu.CompilerParams(
            dimension_semantics=("parallel", "arbitrary"), vmem_limit_bytes=VMEM_LIMIT),
        name="ffn",
    )(h, gpre, w_in, w_in, w_out, gpost)


def _rope(x, c, sa, sb):
    half = ROT_DIM // 2
    return x * c + pltpu.roll(x, LANES - half, 1) * sa + pltpu.roll(x, half, 1) * sb


def _mix_in_kernel(h_ref, gpre_ref, w_ref, c_ref, sa_ref, sb_ref, kig_ref, smscale_ref,
                   qkva_ref, z_ref, qb_ref, kb_ref, vb_ref, qi_ref, g_ref, small_ref):
    xn = (_rms(h_ref[...]) * gpre_ref[...]).astype(BF16)
    c, sa, sb = c_ref[...], sa_ref[...], sb_ref[...]
    W = GROUP_W

    def proj(lo, hi):
        return _dot(xn, w_ref[:, lo:hi])

    qkva_ref[...] = proj(0, 3 * W)
    z_ref[...] = proj(3 * W, 4 * W)

    def roped(lo, scale):
        y = proj(lo, lo + W)
        parts = [_rope(y[:, t * LANES:(t + 1) * LANES], c, sa, sb) for t in range(W // LANES)]
        return (jnp.concatenate(parts, axis=1) * scale).astype(BF16)

    inv_sqrt_d = HEAD_DIM ** -0.5
    qb_ref[...] = roped(4 * W, inv_sqrt_d * LOG2_E)
    kb_ref[...] = roped(5 * W, 1.0)
    vb_ref[...] = proj(6 * W, 7 * W).astype(BF16)
    qi_ref[...] = roped(7 * W, inv_sqrt_d)
    g_ref[...] = proj(8 * W, 12 * W)

    sm = proj(12 * W, 12 * W + LANES)
    lane = lax.broadcasted_iota(I32, sm.shape, 1)
    is_ki = lane < HEAD_DIM
    ms = jnp.sum(jnp.where(is_ki, sm * sm, 0.0), axis=-1, keepdims=True) * (1.0 / HEAD_DIM)
    ki = _rope(sm * lax.rsqrt(ms + EPS) * kig_ref[...], c, sa, sb)
    small_ref[...] = jnp.where(is_ki, ki, sm * smscale_ref[...])


def _mix_in(h, gpre, w, rope_c, rope_sa, rope_sb, kig, smscale, *, tm, seq):
    T, D = h.shape
    W = GROUP_W
    ncol = w.shape[1]
    nseq = seq // tm
    row = lambda i: (i, 0)
    const = lambda i: (0, 0)
    pos = lambda i: (i % nseq, 0)
    outs = [(3 * W, F32), (W, F32), (W, BF16), (W, BF16), (W, BF16), (W, BF16), (4 * W, F32), (LANES, F32)]
    return pl.pallas_call(
        _mix_in_kernel,
        grid=(T // tm,),
        in_specs=[
            pl.BlockSpec((tm, D), row),
            pl.BlockSpec((1, D), const),
            pl.BlockSpec((D, ncol), const),
            pl.BlockSpec((tm, LANES), pos),
            pl.BlockSpec((tm, LANES), pos),
            pl.BlockSpec((tm, LANES), pos),
            pl.BlockSpec((1, LANES), const),
            pl.BlockSpec((1, LANES), const),
        ],
        out_specs=[pl.BlockSpec((tm, n), row) for n, _ in outs],
        out_shape=[jax.ShapeDtypeStruct((T, n), dt) for n, dt in outs],
        compiler_params=pltpu.CompilerParams(
            dimension_semantics=("parallel",), vmem_limit_bytes=VMEM_LIMIT),
        name="mix_in",
    )(h, gpre, w, rope_c, rope_sa, rope_sb, kig, smscale)


def _gdn_kernel(qkv_ref, z_ref, small_ref, convw_ref, alog_ref, dtb_ref, alogc_ref, dtbc_ref,
                dng_ref, bd_ref, o_ref,
                xs_ref, q3_ref, k3_ref, v3_ref, t_ref, pw_ref, qkd_ref, rhs_ref, u_ref, wd_ref,
                qd_ref, kdT_ref, oc_ref, state_ref, *, P, C):
    W = GROUP_W
    HD = HEAD_DIM
    H = N_HEADS
    shift = int(math.log2(C))
    i = pl.program_id(1)

    @pl.when(i == 0)
    def _():
        xs_ref[0:8, :] = jnp.zeros((8, 3 * W), F32)
        state_ref[...] = jnp.zeros_like(state_ref)

    xs_ref[8:8 + P, :] = qkv_ref[...]
    w = convw_ref[...]
    y = w[CONV_TAPS - 1:CONV_TAPS, :] * xs_ref[8:8 + P, :]
    for tap in range(CONV_TAPS - 1):
        off = 8 - (CONV_TAPS - 1) + tap
        y = y + w[tap:tap + 1, :] * xs_ref[off:off + P, :]
    xs_ref[0:8, :] = xs_ref[P:P + 8, :]
    y = y * _sigmoid(y)
    q, k, v = y[:, :W], y[:, W:2 * W], y[:, 2 * W:]

    bd = bd_ref[...]

    def head_sum(x):
        hi, lo = _split2(x)
        return _dot(hi, bd) + _dot(lo, bd)

    qn = q * lax.rsqrt(head_sum(q * q) + EPS) * (HD ** -0.5)
    kn = k * lax.rsqrt(head_sum(k * k) + EPS)
    for h in range(H):
        sl = slice(h * HD, (h + 1) * HD)
        q3_ref[h] = qn[:, sl]
        k3_ref[h] = kn[:, sl]
        v3_ref[h] = v[:, sl]

    sm = small_ref[...]
    gfull = -jnp.exp(alog_ref[...]) * _softplus(sm + dtb_ref[...])
    bfull = _sigmoid(sm)
    gT = -jnp.exp(alogc_ref[...]) * _softplus(sm.T[SM_A:SM_A + H, :] + dtbc_ref[...])

    ri = lax.broadcasted_iota(I32, (P, P), 0)
    ci = lax.broadcasted_iota(I32, (P, P), 1)
    same = lax.shift_right_logical(ri, shift) == lax.shift_right_logical(ci, shift)
    incl = jnp.logical_and(same, ri >= ci)
    diag = ri == ci
    tri = jnp.where(incl, 1.0, 0.0).astype(BF16)
    triT = jnp.where(same, jnp.where(ci >= ri, 1.0, 0.0), 0.0).astype(BF16)
    blk = jnp.where(same, 1.0, 0.0).astype(BF16)

    g_parts = _split3(gfull)
    gc = sum(_dot(tri, part) for part in g_parts)
    gtot = sum(_dot(blk, part) for part in g_parts)
    gcT = sum(_dot(part, triT) for part in _split3(gT))
    egc = jnp.exp(gc)
    ekd = jnp.exp(gtot - gc)

    for h in range(H):
        K, Q, V = k3_ref[h], q3_ref[h], v3_ref[h]
        Kb = K.astype(BF16)
        gcol = gc[:, SM_A + h:SM_A + h + 1]
        grow = gcT[h:h + 1, :]
        bcol = bfull[:, SM_B + h:SM_B + h + 1]
        ecol = egc[:, SM_A + h:SM_A + h + 1]
        decay = jnp.where(incl, jnp.exp(gcol - grow), 0.0)
        n_mat = jnp.where(diag, 0.0, -(bcol * _dot_nt(Kb, Kb) * decay))
        t_ref[h] = jnp.where(diag, 1.0, n_mat)
        pw_ref[0, h] = n_mat.astype(BF16)
        qkd_ref[h] = (_dot_nt(Q.astype(BF16), Kb) * decay).astype(BF16)
        rhs_ref[h] = jnp.concatenate([V * bcol, K * (bcol * ecol)], axis=1).astype(BF16)
        qd_ref[h] = (Q * ecol).astype(BF16)
        kdT_ref[h] = (K * ekd[:, SM_A + h:SM_A + h + 1]).T.astype(BF16)

    for it in range(shift - 1):
        src, dst = it % 2, 1 - it % 2
        for h in range(H):
            pw = pw_ref[src, h]
            pw_ref[dst, h] = _dot(pw, pw).astype(BF16)
        for h in range(H):
            t = t_ref[h]
            t_ref[h] = t + _dot(t.astype(BF16), pw_ref[dst, h])
    for h in range(H):
        sol = _dot(t_ref[h].astype(BF16), rhs_ref[h])
        u_ref[h] = sol[:, :HD]
        wd_ref[h] = sol[:, HD:].astype(BF16)

    for c in range(P // C):
        rows = slice(c * C, (c + 1) * C)
        states = [state_ref[h] for h in range(H)]
        sbf = [s.astype(BF16) for s in states]
        v_new = [(u_ref[h, rows, :] - _dot(wd_ref[h, rows, :], sbf[h])).astype(BF16) for h in range(H)]
        for h in range(H):
            oc_ref[rows, h * HD:(h + 1) * HD] = (
                _dot(qd_ref[h, rows, :], sbf[h]) + _dot(qkd_ref[h, rows, rows], v_new[h]))
        for h in range(H):
            g_tot = jnp.exp(gcT[h:h + 1, (c + 1) * C - 1:(c + 1) * C])
            state_ref[h] = states[h] * g_tot + _dot(kdT_ref[h, :, rows], v_new[h])

    o = oc_ref[...]
    ms = head_sum(o * o) * (1.0 / HD)
    zz = z_ref[...]
    o_ref[...] = (o * lax.rsqrt(ms + EPS) * dng_ref[...] * (zz * _sigmoid(zz))).astype(BF16)


def _gdn(qkva, z, small, convw, alog_row, dtb_row, alog_col, dtb_col, dng, bd, *, batch, seq, P, C):
    T = qkva.shape[0]
    W = GROUP_W
    H, HD = N_HEADS, HEAD_DIM
    nblk = seq // P
    row = lambda b, i: (b * nblk + i, 0)
    const = lambda b, i: (0, 0)
    return pl.pallas_call(
        functools.partial(_gdn_kernel, P=P, C=C),
        grid=(batch, nblk),
        in_specs=[
            pl.BlockSpec((P, 3 * W), row),
            pl.BlockSpec((P, W), row),
            pl.BlockSpec((P, LANES), row),
            pl.BlockSpec((CONV_TAPS, 3 * W), const),
            pl.BlockSpec((1, LANES), const),
            pl.BlockSpec((1, LANES), const),
            pl.BlockSpec((H, 1), const),
            pl.BlockSpec((H, 1), const),
            pl.BlockSpec((1, W), const),
            pl.BlockSpec((W, W), const),
        ],
        out_specs=pl.BlockSpec((P, W), row),
        out_shape=jax.ShapeDtypeStruct((T, W), BF16),
        scratch_shapes=[
            pltpu.VMEM((P + 8, 3 * W), F32),
            pltpu.VMEM((H, P, HD), F32),
            pltpu.VMEM((H, P, HD), F32),
            pltpu.VMEM((H, P, HD), F32),
            pltpu.VMEM((H, P, P), F32),
            pltpu.VMEM((2, H, P, P), BF16),
            pltpu.VMEM((H, P, P), BF16),
            pltpu.VMEM((H, P, 2 * HD), BF16),
            pltpu.VMEM((H, P, HD), F32),
            pltpu.VMEM((H, P, HD), BF16),
            pltpu.VMEM((H, P, HD), BF16),
            pltpu.VMEM((H, HD, P), BF16),
            pltpu.VMEM((P, W), F32),
            pltpu.VMEM((H, HD, HD), F32),
        ],
        compiler_params=pltpu.CompilerParams(
            dimension_semantics=("parallel", "arbitrary"), vmem_limit_bytes=VMEM_LIMIT),
        name="gdn",
    )(qkva, z, small, convw, alog_row, dtb_row, alog_col, dtb_col, dng, bd)


def _key_to_f32(key):
    bits = key ^ (lax.shift_right_arithmetic(key, 31) & 0x7FFFFFFF)
    return pltpu.bitcast(bits, F32)


def _f32_to_key(x):
    bits = pltpu.bitcast(x, I32)
    return bits ^ (lax.shift_right_arithmetic(bits, 31) & 0x7FFFFFFF)


def _dsa_kernel(qb_ref, qi_ref, wsm_ref, kb_ref, vb_ref, ksm_ref, o_ref,
                sc_ref, vT_ref, kib_ref, s_ref, acc_ref, knorm_ref, *, Tq, Tk, seq, topk):
    HD = HEAD_DIM
    H = N_HEADS
    W = GROUP_W
    ACC = 4 * SUBLANES
    i = pl.program_id(1)
    nkt = ((i + 1) * Tq + Tk - 1) // Tk
    qpos = i * Tq + lax.broadcasted_iota(I32, (1, Tq), 1)
    sub_pos = lax.broadcasted_iota(I32, (Tk, Tq), 0)
    head_rows = lax.broadcasted_iota(I32, (H, W), 0)
    head_of_lane = lax.shift_right_logical(lax.broadcasted_iota(I32, (H, W), 1), int(math.log2(HD)))
    head_sel = jnp.where(head_rows == head_of_lane, 1.0, 0.0).astype(BF16)

    @pl.when(i == 0)
    def _():
        k_norm2 = jnp.zeros((H, 1), F32)
        for kt in range(seq // Tk):
            rows = slice(kt * Tk, (kt + 1) * Tk)
            vT_ref[kt] = vb_ref[rows, :].astype(F32).T.astype(BF16)
            k_tile = kb_ref[rows, :].astype(F32)
            per_head = _dot_nt(head_sel, (k_tile * k_tile).astype(BF16))
            k_norm2 = jnp.maximum(k_norm2, jnp.max(per_head, axis=1, keepdims=True))
        knorm_ref[...] = jnp.broadcast_to(jnp.sqrt(k_norm2), knorm_ref.shape)
        kib_ref[...] = ksm_ref[...].astype(BF16)

    wT = wsm_ref[...].T
    qi_heads = [qi_ref[:, h * HD:(h + 1) * HD] for h in range(H)]
    w_rows = [wT[SM_WI + h:SM_WI + h + 1, :] for h in range(H)]

    def fold(m):
        return jnp.sum(m.reshape(Tk // ACC, ACC, Tq), axis=0)

    def score_body(kt, carry):
        rmax, rmin, ge0, gt0 = carry
        r0 = pl.multiple_of(kt * Tk, Tk)
        ki = kib_ref[pl.ds(r0, Tk), :][:, SM_KI:SM_KI + HD]
        dots = [_dot_nt(ki, qi_heads[h]) for h in range(H)]
        score = jnp.zeros((Tk, Tq), F32)
        for h in range(H):
            score = score + w_rows[h] * jnp.maximum(dots[h], 0.0)
        adm = r0 + sub_pos <= qpos
        sc = jnp.where(adm, score, -jnp.inf)
        sc_ref[pl.ds(r0, Tk), :] = sc
        rmax = jnp.maximum(rmax, jnp.max(sc, axis=0, keepdims=True))
        rmin = jnp.minimum(rmin, jnp.min(jnp.where(adm, score, jnp.inf), axis=0, keepdims=True))
        ge0 = ge0 + fold(jnp.where(sc >= 0.0, 1.0, 0.0))
        gt0 = gt0 + fold(jnp.where(sc > 0.0, 1.0, 0.0))
        return rmax, rmin, ge0, gt0

    row = lambda v: jnp.full((1, Tq), v, F32)
    rmax, rmin, ge0, gt0 = lax.fori_loop(
        0, nkt, score_body, (row(-jnp.inf), row(jnp.inf), jnp.zeros((ACC, Tq), F32), jnp.zeros((ACC, Tq), F32)))
    cnt_ge0 = jnp.sum(ge0, axis=0, keepdims=True)
    cnt_gt0 = jnp.sum(gt0, axis=0, keepdims=True)

    def count(pred):
        def body(kt, acc):
            r0 = pl.multiple_of(kt * Tk, Tk)
            return acc + fold(pred(r0, sc_ref[pl.ds(r0, Tk), :]))
        acc = lax.fori_loop(0, nkt, body, jnp.zeros((ACC, Tq), F32))
        return jnp.sum(acc, axis=0, keepdims=True)

    def count_ge(v):
        return count(lambda r0, s: jnp.where(s >= v, 1.0, 0.0))

    def count_gt(v):
        return count(lambda r0, s: jnp.where(s > v, 1.0, 0.0))

    kf = float(topk)
    log_k = math.log(kf)
    short = qpos < topk
    zero_tie = jnp.logical_and(cnt_gt0 < kf, cnt_ge0 >= kf)
    positive = cnt_gt0 >= kf
    tiny = row(FLT_MIN_NORMAL)
    lo_k = jnp.where(positive, _f32_to_key(tiny), _f32_to_key(rmin))
    hi_k = jnp.where(positive, _f32_to_key(rmax) + 1, _f32_to_key(-tiny))
    c_lo = jnp.where(positive, cnt_gt0, (qpos + 1).astype(F32))
    c_hi = jnp.where(positive, 0.0, cnt_ge0)
    done0 = jnp.where(jnp.logical_or(jnp.logical_or(short, zero_tie), c_lo == kf), 1.0, 0.0)

    def probe(st):
        it, lo_k, hi_k, c_lo, c_hi, w_lo, w_hi, last, done = st
        lo = _key_to_f32(lo_k)
        hi = _key_to_f32(hi_k - 1)
        a = (jnp.log(c_lo + 0.5) - log_k) * w_lo
        b = (log_k - jnp.log(c_hi + 0.5)) * w_hi
        frac = jnp.clip(a / jnp.maximum(a + b, 1e-30), 0.02, 0.98)
        cand_k = _f32_to_key(lo + frac * (hi - lo))
        mid_k = lo_k + lax.shift_right_logical(hi_k - lo_k, 1)
        cand_k = jnp.where(jnp.logical_and(it >= SEARCH_FAST_PASSES, it % 4 == 3), mid_k, cand_k)
        cand_k = jnp.minimum(jnp.maximum(cand_k, lo_k + 1), hi_k - 1)
        c = count_ge(_key_to_f32(cand_k))
        live = done == 0.0
        up = jnp.logical_and(live, c >= kf)
        dn = jnp.logical_and(live, c < kf)
        w_hi = jnp.where(jnp.logical_and(up, last > 0.0), w_hi * 0.5, jnp.where(dn, 1.0, w_hi))
        w_lo = jnp.where(jnp.logical_and(dn, last < 0.0), w_lo * 0.5, jnp.where(up, 1.0, w_lo))
        last = jnp.where(up, 1.0, jnp.where(dn, -1.0, last))
        lo_k = jnp.where(up, cand_k, lo_k)
        c_lo = jnp.where(up, c, c_lo)
        hi_k = jnp.where(dn, cand_k, hi_k)
        c_hi = jnp.where(dn, c, c_hi)
        done = jnp.where(jnp.logical_or(c_lo == kf, hi_k - lo_k <= 1), 1.0, done)
        return it + 1, lo_k, hi_k, c_lo, c_hi, w_lo, w_hi, last, done

    def search_cond(st):
        return jnp.logical_and(st[0] < SEARCH_MAX_PASSES, jnp.min(st[-1]) == 0.0)

    st = (jnp.int32(0), lo_k, hi_k, c_lo, c_hi, row(1.0), row(1.0), row(0.0), done0)
    st = lax.while_loop(search_cond, lambda st: probe(probe(st)), st)
    _, lo_k, hi_k, c_lo, c_hi = st[:5]

    lo_f = jnp.where(short, -jnp.inf, jnp.where(zero_tie, 0.0, _key_to_f32(lo_k)))
    hi_f = jnp.where(zero_tie, tiny, _key_to_f32(hi_k))
    c_lo = jnp.where(zero_tie, cnt_ge0, c_lo)
    c_hi = jnp.where(zero_tie, cnt_gt0, c_hi)
    cnt_gt = count_gt(lo_f)

    def unresolved(lo_f, hi_f, c_lo, c_hi, cnt_gt):
        inside = c_lo - c_hi
        u = jnp.logical_and(kf - c_hi < inside, c_lo - cnt_gt < inside)
        return jnp.where(short, 0.0, jnp.where(u, 1.0, 0.0))

    def refine_cond(st):
        return jnp.logical_and(st[0] < SEARCH_MAX_PASSES, jnp.max(unresolved(*st[1:])) > 0.0)

    def refine(st):
        it, lo_f, hi_f, c_lo, c_hi, cnt_gt = st
        live = unresolved(lo_f, hi_f, c_lo, c_hi, cnt_gt) > 0.0
        mid = 0.5 * lo_f + 0.5 * hi_f
        c = count_ge(mid)
        up = jnp.logical_and(live, c >= kf)
        dn = jnp.logical_and(live, c < kf)
        lo_f = jnp.where(up, mid, lo_f)
        c_lo = jnp.where(up, c, c_lo)
        hi_f = jnp.where(dn, mid, hi_f)
        c_hi = jnp.where(dn, c, c_hi)
        return it + 1, lo_f, hi_f, c_lo, c_hi, count_gt(lo_f)

    _, thr, _, c_thr, _, cnt_gt = lax.while_loop(
        refine_cond, refine, (jnp.int32(0), lo_f, hi_f, c_lo, c_hi, cnt_gt))

    need = kf - cnt_gt
    cut = jnp.where(short, 0.0, jnp.where(c_thr - cnt_gt > need, 1.0, 0.0))
    any_cut = jnp.max(cut) > 0.0
    nbits = max(1, int(math.ceil(math.log2(seq))))

    def idx_body(t, p):
        cand = p | lax.shift_left(jnp.int32(1), nbits - 1 - t)

        def pred(r0, s):
            return jnp.where(s == thr, jnp.where(r0 + sub_pos < cand, 1.0, 0.0), 0.0)
        return jnp.where(count(pred) < need, cand, p)

    p_init = jnp.where(any_cut, 0, 2 ** nbits - 1) + jnp.zeros((1, Tq), I32)
    p_cut = lax.fori_loop(0, jnp.where(any_cut, nbits, 0), idx_body, p_init)
    p_cut = jnp.where(short, -1, p_cut)

    def mask_body(kt, carry):
        r0 = pl.multiple_of(kt * Tk, Tk)
        sc = sc_ref[pl.ds(r0, Tk), :]
        tie_ok = jnp.where(r0 + sub_pos <= p_cut, 0.0, -jnp.inf)
        sc_ref[pl.ds(r0, Tk), :] = jnp.where(sc > thr, 0.0, jnp.where(sc == thr, tie_ok, -jnp.inf))
        return carry

    lax.fori_loop(0, nkt, mask_body, 0)

    q_all = qb_ref[...].astype(F32)
    q_norm = jnp.sqrt(_dot_nt(head_sel, (q_all * q_all).astype(BF16)))
    shift = [q_norm[h:h + 1, :] * (ATT_BOUND_MARGIN * knorm_ref[h:h + 1, 0:1]) for h in range(H)]

    def att_fast(kt, l_all):
        r0 = pl.multiple_of(kt * Tk, Tk)
        bias = sc_ref[pl.ds(r0, Tk), :]
        for h in range(H):
            sl = slice(h * HD, (h + 1) * HD)
            s_ref[h] = _dot_nt(kb_ref[pl.ds(r0, Tk), sl], qb_ref[:, sl]) + bias - shift[h]
        l_rows = []
        for h in range(H):
            p = jnp.exp2(s_ref[h])
            l_rows.append(l_all[h:h + 1, :] + jnp.sum(p, axis=0, keepdims=True))
            acc_ref[h] += _dot(vT_ref[kt, h * HD:(h + 1) * HD, :], p.astype(BF16))
        return jnp.concatenate(l_rows, axis=0)

    acc_ref[...] = jnp.zeros_like(acc_ref)
    l_fast = lax.fori_loop(0, nkt, att_fast, jnp.zeros((H, Tq), F32))

    def att_slow(kt, carry):
        m_all, l_all = carry
        r0 = pl.multiple_of(kt * Tk, Tk)
        bias = sc_ref[pl.ds(r0, Tk), :]
        tile_max = []
        for h in range(H):
            sl = slice(h * HD, (h + 1) * HD)
            s = _dot_nt(kb_ref[pl.ds(r0, Tk), sl], qb_ref[:, sl]) + bias
            s_ref[h] = s
            tile_max.append(jnp.max(s, axis=0, keepdims=True))
        m_rows, l_rows = [], []
        for h in range(H):
            s = s_ref[h]
            m_old = m_all[h:h + 1, :]
            m_new = jnp.maximum(m_old, tile_max[h])
            alpha = jnp.exp2(m_old - m_new)
            p = jnp.exp2(s - m_new)
            l_rows.append(alpha * l_all[h:h + 1, :] + jnp.sum(p, axis=0, keepdims=True))
            acc_ref[h] = alpha * acc_ref[h] + _dot(vT_ref[kt, h * HD:(h + 1) * HD, :], p.astype(BF16))
            m_rows.append(m_new)
        return jnp.concatenate(m_rows, axis=0), jnp.concatenate(l_rows, axis=0)

    def redo():
        acc_ref[...] = jnp.zeros_like(acc_ref)
        init = (jnp.full((H, Tq), -1e30, F32), jnp.zeros((H, Tq), F32))
        return lax.fori_loop(0, nkt, att_slow, init)[1]

    l_fin = lax.cond(jnp.min(l_fast) > ATT_MIN_DENOM, lambda: l_fast, redo)
    out_t = jnp.concatenate([acc_ref[h] / l_fin[h:h + 1, :] for h in range(H)], axis=0)
    o_ref[...] = out_t.T.astype(BF16)


def _dsa(qb, kb, vb, qi, small, *, batch, seq, Tq, Tk):
    T = qb.shape[0]
    W = GROUP_W
    nq = seq // Tq
    qrow = lambda b, i: (b * nq + i, 0)
    full = lambda b, i: (b, 0)
    topk = min(TOPK_MAX, seq // 4)
    return pl.pallas_call(
        functools.partial(_dsa_kernel, Tq=Tq, Tk=Tk, seq=seq, topk=topk),
        grid=(batch, nq),
        in_specs=[
            pl.BlockSpec((Tq, W), qrow),
            pl.BlockSpec((Tq, W), qrow),
            pl.BlockSpec((Tq, LANES), qrow),
            pl.BlockSpec((seq, W), full),
            pl.BlockSpec((seq, W), full),
            pl.BlockSpec((seq, LANES), full),
        ],
        out_specs=pl.BlockSpec((Tq, W), qrow),
        out_shape=jax.ShapeDtypeStruct((T, W), BF16),
        scratch_shapes=[
            pltpu.VMEM((seq, Tq), F32),
            pltpu.VMEM((seq // Tk, W, Tk), BF16),
            pltpu.VMEM((seq, LANES), BF16),
            pltpu.VMEM((N_HEADS, Tk, Tq), F32),
            pltpu.VMEM((N_HEADS, HEAD_DIM, Tq), F32),
            pltpu.VMEM((N_HEADS, LANES), F32),
        ],
        compiler_params=pltpu.CompilerParams(
            dimension_semantics=("parallel", "arbitrary"), vmem_limit_bytes=VMEM_LIMIT),
        name="dsa",
    )(qb, qi, small, kb, vb, small)


def _merge_kernel(h_ref, oa_ref, ob_ref, g_ref, wa_ref, wb_ref, wo_ref, gpost_ref, o_ref):
    D = h_ref.shape[1]
    ya = _dot(oa_ref[...], wa_ref[...])
    yb = _dot(ob_ref[...], wb_ref[...])
    g = g_ref[...]
    merged = _sigmoid(g[:, :D]) * ya + _sigmoid(g[:, D:]) * yb
    mix = _dot(merged.astype(BF16), wo_ref[...])
    o_ref[...] = h_ref[...] + _rms(mix) * gpost_ref[...]


def _merge(h, oa, ob, g, wa, wb, wo, gpost, *, tm):
    T, D = h.shape
    W = GROUP_W
    row = lambda i: (i, 0)
    const = lambda i: (0, 0)
    return pl.pallas_call(
        _merge_kernel,
        grid=(T // tm,),
        in_specs=[
            pl.BlockSpec((tm, D), row),
            pl.BlockSpec((tm, W), row),
            pl.BlockSpec((tm, W), row),
            pl.BlockSpec((tm, 2 * D), row),
            pl.BlockSpec((W, D), const),
            pl.BlockSpec((W, D), const),
            pl.BlockSpec((D, D), const),
            pl.BlockSpec((1, D), const),
        ],
        out_specs=pl.BlockSpec((tm, D), row),
        out_shape=jax.ShapeDtypeStruct((T, D), F32),
        compiler_params=pltpu.CompilerParams(
            dimension_semantics=("parallel",), vmem_limit_bytes=VMEM_LIMIT),
        name="merge",
    )(h, oa, ob, g, wa, wb, wo, gpost)


def _ple_kernel(h_ref, p_ref, gpre_ref, wg_ref, wp_ref, gpost_ref, o_ref):
    h = h_ref[...]
    gate = _sigmoid(_dot((_rms(h) * gpre_ref[...]).astype(BF16), wg_ref[...]))
    e = _dot(p_ref[...].astype(BF16), wp_ref[...])
    o_ref[...] = h + _rms(gate * e) * gpost_ref[...]


def _ple(h, p, gpre, wg, wp, gpost, *, tm):
    T, D = h.shape
    P = p.shape[1]
    row = lambda i: (i, 0)
    const = lambda i: (0, 0)
    return pl.pallas_call(
        _ple_kernel,
        grid=(T // tm,),
        in_specs=[
            pl.BlockSpec((tm, D), row),
            pl.BlockSpec((tm, P), row),
            pl.BlockSpec((1, D), const),
            pl.BlockSpec((D, D), const),
            pl.BlockSpec((P, D), const),
            pl.BlockSpec((1, D), const),
        ],
        out_specs=pl.BlockSpec((tm, D), row),
        out_shape=jax.ShapeDtypeStruct((T, D), F32),
        compiler_params=pltpu.CompilerParams(
            dimension_semantics=("parallel",), vmem_limit_bytes=VMEM_LIMIT),
        name="ple",
    )(h, p, gpre, wg, wp, gpost)


def _rope_tables(seq):
    half = ROT_DIM // 2
    pos = jnp.arange(seq, dtype=F32)
    inv = ROPE_THETA ** (-jnp.arange(0, ROT_DIM, 2, dtype=F32) / ROT_DIM)
    ang = pos[:, None] * inv[None, :]
    cos, sin = jnp.cos(ang), jnp.sin(ang)
    ones = jnp.ones((seq, HEAD_DIM - ROT_DIM), F32)
    zeros_h = jnp.zeros((seq, half), F32)
    zeros_r = jnp.zeros((seq, HEAD_DIM - ROT_DIM), F32)
    c = jnp.concatenate([cos, cos, ones], axis=1)
    sa = jnp.concatenate([-sin, zeros_h, zeros_r], axis=1)
    sb = jnp.concatenate([zeros_h, sin, zeros_r], axis=1)
    rep = LANES // HEAD_DIM
    return jnp.tile(c, (1, rep)), jnp.tile(sa, (1, rep)), jnp.tile(sb, (1, rep))


def _pad_lanes(vec, offset):
    out = jnp.zeros((1, LANES), F32)
    return lax.dynamic_update_slice(out, vec.reshape(1, -1).astype(F32), (0, offset))


def kernel(x, p, ffn1_norm_pre, ffn1_norm_post, ffn1_w_in, ffn1_w_out, mix_norm_pre, mix_norm_post, mix_w_in, conv_w, a_log, dt_bias, dn_norm_g, idx_k_norm_g, w_br_a, w_br_b, mix_w_out, ffn2_norm_pre, ffn2_norm_post, ffn2_w_in, ffn2_w_out, ple_norm_pre, ple_norm_post, ple_w_gate, ple_w_proj):
    B, S, D = x.shape
    T = B * S
    W = GROUP_W
    depth = ffn1_w_in.shape[0]
    ff = ffn1_w_out.shape[1]

    tm_ffn = min(512, T)
    tf = ff // 2 if (ff // 2) % LANES == 0 else ff
    tm_mix = min(256, S)
    gdn_rows = min(256, S)
    gdn_chunk = min(64, gdn_rows)
    Tq = min(512, S)
    Tk = min(512, S)
    tm_post = min(512, T)

    rope_c, rope_sa, rope_sb = _rope_tables(S)
    ri = jnp.arange(W)
    bd = (ri[:, None] // HEAD_DIM == ri[None, :] // HEAD_DIM).astype(BF16)
    smscale = _pad_lanes(jnp.full((N_HEADS,), N_HEADS ** -0.5, F32), SM_WI)
    smscale = jnp.where(smscale == 0.0, 1.0, smscale)
    vec = lambda a: a.reshape(1, -1).astype(F32)

    h = x.reshape(T, D)
    for l in range(depth):
        h = _ffn(h, vec(ffn1_norm_pre[l]), ffn1_w_in[l].astype(BF16), ffn1_w_out[l].astype(BF16),
                 vec(ffn1_norm_post[l]), tm=tm_ffn, tf=tf)

        wi = mix_w_in[l]
        o_a, o_b = 4 * W, 4 * W + 2 * N_HEADS
        o_ki = o_b + 4 * W
        o_wi = o_ki + HEAD_DIM
        o_g = o_wi + N_HEADS
        w_cat = jnp.concatenate([
            wi[:, :4 * W], wi[:, o_b:o_ki], wi[:, o_g:o_g + 2 * D],
            wi[:, o_ki:o_g], wi[:, o_a:o_b],
            jnp.zeros((D, LANES - HEAD_DIM - 3 * N_HEADS), wi.dtype)], axis=1).astype(BF16)
        qkva, z, qb, kb, vb, qi, g, small = _mix_in(
            h, vec(mix_norm_pre[l]), w_cat, rope_c, rope_sa, rope_sb,
            _pad_lanes(idx_k_norm_g[l], SM_KI), smscale, tm=tm_mix, seq=S)

        o_gdn = _gdn(qkva, z, small, conv_w[l].astype(F32),
                     _pad_lanes(a_log[l], SM_A), _pad_lanes(dt_bias[l], SM_A),
                     a_log[l].reshape(N_HEADS, 1).astype(F32), dt_bias[l].reshape(N_HEADS, 1).astype(F32),
                     jnp.tile(vec(dn_norm_g[l]), (1, N_HEADS)), bd,
                     batch=B, seq=S, P=gdn_rows, C=gdn_chunk)
        o_dsa = _dsa(qb, kb, vb, qi, small, batch=B, seq=S, Tq=Tq, Tk=Tk)

        h = _merge(h, o_gdn, o_dsa, g, w_br_a[l].astype(BF16), w_br_b[l].astype(BF16),
                   mix_w_out[l].astype(BF16), vec(mix_norm_post[l]), tm=tm_post)
        h = _ffn(h, vec(ffn2_norm_pre[l]), ffn2_w_in[l].astype(BF16), ffn2_w_out[l].astype(BF16),
                 vec(ffn2_norm_post[l]), tm=tm_ffn, tf=tf)
        h = _ple(h, p[l].reshape(T, -1), vec(ple_norm_pre[l]), ple_w_gate[l].astype(BF16),
                 ple_w_proj[l].astype(BF16), vec(ple_norm_post[l]), tm=tm_post)
    return h.reshape(B, S, D)
```

```python
import functools
import math

import jax
import jax.numpy as jnp
from jax import lax
from jax.experimental import pallas as pl
from jax.experimental.pallas import tpu as pltpu

F32 = jnp.float32
BF16 = jnp.bfloat16
I32 = jnp.int32

EPS = 1e-6
N_HEADS = 8
HEAD_DIM = 64
GROUP_W = N_HEADS * HEAD_DIM
CONV_TAPS = 4
ROT_DIM = HEAD_DIM // 4
ROPE_THETA = 500000.0
TOPK_MAX = 256
LANES = 128
SUBLANES = 8
FLT_MIN_NORMAL = 1.1754943508222875e-38
LOG2_E = 1.4426950408889634
SEARCH_FAST_PASSES = 24
SEARCH_MAX_PASSES = 256
ATT_BOUND_MARGIN = 1.02
ATT_MIN_DENOM = 1e-30
VMEM_LIMIT = 56 * 1024 * 1024

SM_KI = 0
SM_WI = 64
SM_A = 72
SM_B = 80


def _rms(x):
    return x * lax.rsqrt(jnp.mean(x * x, axis=-1, keepdims=True) + EPS)


def _dot(a, b):
    return jnp.dot(a, b, preferred_element_type=F32)


def _dot_nt(a, b):
    return lax.dot_general(a, b, (((1,), (1,)), ((), ())), preferred_element_type=F32)


def _split2(x):
    hi = x.astype(BF16)
    lo = (x - hi.astype(F32)).astype(BF16)
    return hi, lo


def _split3(x):
    hi = x.astype(BF16)
    r = x - hi.astype(F32)
    mid = r.astype(BF16)
    lo = (r - mid.astype(F32)).astype(BF16)
    return hi, mid, lo


def _softplus(x):
    return jnp.maximum(x, 0.0) + jnp.log(1.0 + jnp.exp(-jnp.abs(x)))


def _sigmoid(x):
    return 1.0 / (1.0 + jnp.exp(-x))


def _ffn_kernel(x_ref, gpre_ref, wg_ref, wu_ref, wo_ref, gpost_ref, o_ref, xn_ref, acc_ref, *, nj):
    j = pl.program_id(1)

    @pl.when(j == 0)
    def _():
        xn_ref[...] = (_rms(x_ref[...]) * gpre_ref[...]).astype(BF16)
        acc_ref[...] = jnp.zeros_like(acc_ref)

    xn = xn_ref[...]
    g = _dot(xn, wg_ref[...])
    u = _dot(xn, wu_ref[...])
    a = (g * _sigmoid(g) * u).astype(BF16)
    acc_ref[...] += _dot(a, wo_ref[...])

    @pl.when(j == nj - 1)
    def _():
        o_ref[...] = x_ref[...] + 0.5 * (_rms(acc_ref[...]) * gpost_ref[...])


def _ffn(h, gpre, w_in, w_out, gpost, *, tm, tf):
    T, D = h.shape
    FF = w_out.shape[0]
    nj = FF // tf
    return pl.pallas_call(
        functools.partial(_ffn_kernel, nj=nj),
        grid=(T // tm, nj),
        in_specs=[
            pl.BlockSpec((tm, D), lambda i, j: (i, 0)),
            pl.BlockSpec((1, D), lambda i, j: (0, 0)),
            pl.BlockSpec((D, tf), lambda i, j: (0, j)),
            pl.BlockSpec((D, tf), lambda i, j: (0, j + nj)),
            pl.BlockSpec((tf, D), lambda i, j: (j, 0)),
            pl.BlockSpec((1, D), lambda i, j: (0, 0)),
        ],
        out_specs=pl.BlockSpec((tm, D), lambda i, j: (i, 0)),
        out_shape=jax.ShapeDtypeStruct((T, D), F32),
        scratch_shapes=[pltpu.VMEM((tm, D), BF16), pltpu.VMEM((tm, D), F32)],
        compiler_params=pltpu.CompilerParams(
            dimension_semantics=("parallel", "arbitrary"), vmem_limit_bytes=VMEM_LIMIT),
        name="ffn",
    )(h, gpre, w_in, w_in, w_out, gpost)


def _rope(x, c, sa, sb):
    half = ROT_DIM // 2
    return x * c + pltpu.roll(x, LANES - half, 1) * sa + pltpu.roll(x, half, 1) * sb


def _mix_in_kernel(h_ref, gpre_ref, w_ref, c_ref, sa_ref, sb_ref, kig_ref, smscale_ref,
                   qkva_ref, z_ref, qb_ref, kb_ref, vb_ref, qi_ref, g_ref, small_ref):
    xn = (_rms(h_ref[...]) * gpre_ref[...]).astype(BF16)
    c, sa, sb = c_ref[...], sa_ref[...], sb_ref[...]
    W = GROUP_W

    def proj(lo, hi):
        return _dot(xn, w_ref[:, lo:hi])

    qkva_ref[...] = proj(0, 3 * W)
    z_ref[...] = proj(3 * W, 4 * W)

    def roped(lo, scale):
        y = proj(lo, lo + W)
        parts = [_rope(y[:, t * LANES:(t + 1) * LANES], c, sa, sb) for t in range(W // LANES)]
        return (jnp.concatenate(parts, axis=1) * scale).astype(BF16)

    inv_sqrt_d = HEAD_DIM ** -0.5
    qb_ref[...] = roped(4 * W, inv_sqrt_d * LOG2_E)
    kb_ref[...] = roped(5 * W, 1.0)
    vb_ref[...] = proj(6 * W, 7 * W).astype(BF16)
    qi_ref[...] = roped(7 * W, inv_sqrt_d)
    g_ref[...] = proj(8 * W, 12 * W)

    sm = proj(12 * W, 12 * W + LANES)
    lane = lax.broadcasted_iota(I32, sm.shape, 1)
    is_ki = lane < HEAD_DIM
    ms = jnp.sum(jnp.where(is_ki, sm * sm, 0.0), axis=-1, keepdims=True) * (1.0 / HEAD_DIM)
    ki = _rope(sm * lax.rsqrt(ms + EPS) * kig_ref[...], c, sa, sb)
    small_ref[...] = jnp.where(is_ki, ki, sm * smscale_ref[...])


def _mix_in(h, gpre, w, rope_c, rope_sa, rope_sb, kig, smscale, *, tm, seq):
    T, D = h.shape
    W = GROUP_W
    ncol = w.shape[1]
    nseq = seq // tm
    row = lambda i: (i, 0)
    const = lambda i: (0, 0)
    pos = lambda i: (i % nseq, 0)
    outs = [(3 * W, F32), (W, F32), (W, BF16), (W, BF16), (W, BF16), (W, BF16), (4 * W, F32), (LANES, F32)]
    return pl.pallas_call(
        _mix_in_kernel,
        grid=(T // tm,),
        in_specs=[
            pl.BlockSpec((tm, D), row),
            pl.BlockSpec((1, D), const),
            pl.BlockSpec((D, ncol), const),
            pl.BlockSpec((tm, LANES), pos),
            pl.BlockSpec((tm, LANES), pos),
            pl.BlockSpec((tm, LANES), pos),
            pl.BlockSpec((1, LANES), const),
            pl.BlockSpec((1, LANES), const),
        ],
        out_specs=[pl.BlockSpec((tm, n), row) for n, _ in outs],
        out_shape=[jax.ShapeDtypeStruct((T, n), dt) for n, dt in outs],
        compiler_params=pltpu.CompilerParams(
            dimension_semantics=("parallel",), vmem_limit_bytes=VMEM_LIMIT),
        name="mix_in",
    )(h, gpre, w, rope_c, rope_sa, rope_sb, kig, smscale)


def _gdn_kernel(qkv_ref, z_ref, small_ref, convw_ref, alog_ref, dtb_ref, alogc_ref, dtbc_ref,
                dng_ref, bd_ref, o_ref,
                xs_ref, q3_ref, k3_ref, v3_ref, pw_ref, qkd_ref, sol_ref, u_ref, wd_ref,
                qd_ref, kdT_ref, oc_ref, state_ref, *, P, C):
    W = GROUP_W
    HD = HEAD_DIM
    H = N_HEADS
    shift = int(math.log2(C))
    i = pl.program_id(1)

    @pl.when(i == 0)
    def _():
        xs_ref[0:8, :] = jnp.zeros((8, 3 * W), F32)
        state_ref[...] = jnp.zeros_like(state_ref)

    xs_ref[8:8 + P, :] = qkv_ref[...]
    w = convw_ref[...]
    y = w[CONV_TAPS - 1:CONV_TAPS, :] * xs_ref[8:8 + P, :]
    for tap in range(CONV_TAPS - 1):
        off = 8 - (CONV_TAPS - 1) + tap
        y = y + w[tap:tap + 1, :] * xs_ref[off:off + P, :]
    xs_ref[0:8, :] = xs_ref[P:P + 8, :]
    y = y * _sigmoid(y)
    q, k, v = y[:, :W], y[:, W:2 * W], y[:, 2 * W:]

    bd = bd_ref[...]

    def head_sum(x):
        return _dot(x.astype(BF16), bd)

    qn = q * lax.rsqrt(head_sum(q * q) + EPS) * (HD ** -0.5)
    kn = k * lax.rsqrt(head_sum(k * k) + EPS)
    for h in range(H):
        sl = slice(h * HD, (h + 1) * HD)
        q3_ref[h] = qn[:, sl]
        k3_ref[h] = kn[:, sl]
        v3_ref[h] = v[:, sl]

    sm = small_ref[...]
    gfull = -jnp.exp(alog_ref[...]) * _softplus(sm + dtb_ref[...])
    bfull = _sigmoid(sm)
    gT = -jnp.exp(alogc_ref[...]) * _softplus(sm.T[SM_A:SM_A + H, :] + dtbc_ref[...])

    ri = lax.broadcasted_iota(I32, (P, P), 0)
    ci = lax.broadcasted_iota(I32, (P, P), 1)
    same = lax.shift_right_logical(ri, shift) == lax.shift_right_logical(ci, shift)
    incl = jnp.logical_and(same, ri >= ci)
    diag = ri == ci
    tri = jnp.where(incl, 1.0, 0.0).astype(BF16)
    triT = jnp.where(same, jnp.where(ci >= ri, 1.0, 0.0), 0.0).astype(BF16)
    blk = jnp.where(same, 1.0, 0.0).astype(BF16)

    g_parts = _split3(gfull)
    gc = sum(_dot(tri, part) for part in g_parts)
    gtot = sum(_dot(blk, part) for part in g_parts)
    gcT = sum(_dot(part, triT) for part in _split3(gT))
    egc = jnp.exp(gc)
    ekd = jnp.exp(gtot - gc)

    for h in range(H):
        K, Q, V = k3_ref[h], q3_ref[h], v3_ref[h]
        Kb = K.astype(BF16)
        gcol = gc[:, SM_A + h:SM_A + h + 1]
        grow = gcT[h:h + 1, :]
        bcol = bfull[:, SM_B + h:SM_B + h + 1]
        ecol = egc[:, SM_A + h:SM_A + h + 1]
        decay = jnp.where(incl, jnp.exp(gcol - grow), 0.0)
        n_mat = jnp.where(diag, 0.0, -(bcol * _dot_nt(Kb, Kb) * decay)).astype(BF16)
        pw_ref[0, h] = n_mat
        qkd_ref[h] = (_dot_nt(Q.astype(BF16), Kb) * decay).astype(BF16)
        rhs = jnp.concatenate([V * bcol, K * (bcol * ecol)], axis=1)
        sol_ref[h] = rhs + _dot(n_mat, rhs.astype(BF16))
        qd_ref[h] = (Q * ecol).astype(BF16)
        kdT_ref[h] = (K * ekd[:, SM_A + h:SM_A + h + 1]).T.astype(BF16)

    for it in range(shift - 1):
        src, dst = it % 2, 1 - it % 2
        for h in range(H):
            pw = pw_ref[src, h]
            pw_ref[dst, h] = _dot(pw, pw).astype(BF16)
        for h in range(H):
            y = sol_ref[h]
            sol_ref[h] = y + _dot(pw_ref[dst, h], y.astype(BF16))
    for h in range(H):
        sol = sol_ref[h]
        u_ref[h] = sol[:, :HD]
        wd_ref[h] = sol[:, HD:].astype(BF16)

    for c in range(P // C):
        rows = slice(c * C, (c + 1) * C)
        states = [state_ref[h] for h in range(H)]
        sbf = [s.astype(BF16) for s in states]
        v_new = [(u_ref[h, rows, :] - _dot(wd_ref[h, rows, :], sbf[h])).astype(BF16) for h in range(H)]
        for h in range(H):
            oc_ref[rows, h * HD:(h + 1) * HD] = (
                _dot(qd_ref[h, rows, :], sbf[h]) + _dot(qkd_ref[h, rows, rows], v_new[h]))
        for h in range(H):
            g_tot = jnp.exp(gcT[h:h + 1, (c + 1) * C - 1:(c + 1) * C])
            state_ref[h] = states[h] * g_tot + _dot(kdT_ref[h, :, rows], v_new[h])

    o = oc_ref[...]
    ms = head_sum(o * o) * (1.0 / HD)
    zz = z_ref[...]
    o_ref[...] = (o * lax.rsqrt(ms + EPS) * dng_ref[...] * (zz * _sigmoid(zz))).astype(BF16)


def _gdn(qkva, z, small, convw, alog_row, dtb_row, alog_col, dtb_col, dng, bd, *, batch, seq, P, C):
    T = qkva.shape[0]
    W = GROUP_W
    H, HD = N_HEADS, HEAD_DIM
    nblk = seq // P
    row = lambda b, i: (b * nblk + i, 0)
    const = lambda b, i: (0, 0)
    return pl.pallas_call(
        functools.partial(_gdn_kernel, P=P, C=C),
        grid=(batch, nblk),
        in_specs=[
            pl.BlockSpec((P, 3 * W), row),
            pl.BlockSpec((P, W), row),
            pl.BlockSpec((P, LANES), row),
            pl.BlockSpec((CONV_TAPS, 3 * W), const),
            pl.BlockSpec((1, LANES), const),
            pl.BlockSpec((1, LANES), const),
            pl.BlockSpec((H, 1), const),
            pl.BlockSpec((H, 1), const),
            pl.BlockSpec((1, W), const),
            pl.BlockSpec((W, W), const),
        ],
        out_specs=pl.BlockSpec((P, W), row),
        out_shape=jax.ShapeDtypeStruct((T, W), BF16),
        scratch_shapes=[
            pltpu.VMEM((P + 8, 3 * W), F32),
            pltpu.VMEM((H, P, HD), F32),
            pltpu.VMEM((H, P, HD), F32),
            pltpu.VMEM((H, P, HD), F32),
            pltpu.VMEM((2, H, P, P), BF16),
            pltpu.VMEM((H, P, P), BF16),
            pltpu.VMEM((H, P, 2 * HD), F32),
            pltpu.VMEM((H, P, HD), F32),
            pltpu.VMEM((H, P, HD), BF16),
            pltpu.VMEM((H, P, HD), BF16),
            pltpu.VMEM((H, HD, P), BF16),
            pltpu.VMEM((P, W), F32),
            pltpu.VMEM((H, HD, HD), F32),
        ],
        compiler_params=pltpu.CompilerParams(
            dimension_semantics=("parallel", "arbitrary"), vmem_limit_bytes=VMEM_LIMIT),
        name="gdn",
    )(qkva, z, small, convw, alog_row, dtb_row, alog_col, dtb_col, dng, bd)


def _key_to_f32(key):
    bits = key ^ (lax.shift_right_arithmetic(key, 31) & 0x7FFFFFFF)
    return pltpu.bitcast(bits, F32)


def _f32_to_key(x):
    bits = pltpu.bitcast(x, I32)
    return bits ^ (lax.shift_right_arithmetic(bits, 31) & 0x7FFFFFFF)


def _dsa_kernel(qb_ref, qi_ref, wsm_ref, kb_ref, vb_ref, ksm_ref, o_ref,
                sc_ref, vT_ref, kib_ref, s_ref, acc_ref, knorm_ref, *, Tq, Tk, seq, topk):
    HD = HEAD_DIM
    H = N_HEADS
    W = GROUP_W
    ACC = 4 * SUBLANES
    i = pl.program_id(1)
    nkt = ((i + 1) * Tq + Tk - 1) // Tk
    qpos = i * Tq + lax.broadcasted_iota(I32, (1, Tq), 1)
    sub_pos = lax.broadcasted_iota(I32, (Tk, Tq), 0)
    head_rows = lax.broadcasted_iota(I32, (H, W), 0)
    head_of_lane = lax.shift_right_logical(lax.broadcasted_iota(I32, (H, W), 1), int(math.log2(HD)))
    head_sel = jnp.where(head_rows == head_of_lane, 1.0, 0.0).astype(BF16)

    @pl.when(i == 0)
    def _():
        k_norm2 = jnp.zeros((H, 1), F32)
        for kt in range(seq // Tk):
            rows = slice(kt * Tk, (kt + 1) * Tk)
            vT_ref[kt] = vb_ref[rows, :].astype(F32).T.astype(BF16)
            k_tile = kb_ref[rows, :].astype(F32)
            per_head = _dot_nt(head_sel, (k_tile * k_tile).astype(BF16))
            k_norm2 = jnp.maximum(k_norm2, jnp.max(per_head, axis=1, keepdims=True))
        knorm_ref[...] = jnp.broadcast_to(jnp.sqrt(k_norm2), knorm_ref.shape)
        kib_ref[...] = ksm_ref[...].astype(BF16)

    wT = wsm_ref[...].T
    qi_heads = [qi_ref[:, h * HD:(h + 1) * HD] for h in range(H)]
    w_rows = [wT[SM_WI + h:SM_WI + h + 1, :] for h in range(H)]

    def fold(m):
        return jnp.sum(m.reshape(Tk // ACC, ACC, Tq), axis=0)

    def score_body(kt, carry):
        rmax, rmin, ge0, gt0 = carry
        r0 = pl.multiple_of(kt * Tk, Tk)
        ki = kib_ref[pl.ds(r0, Tk), :][:, SM_KI:SM_KI + HD]
        dots = [_dot_nt(ki, qi_heads[h]) for h in range(H)]
        score = jnp.zeros((Tk, Tq), F32)
        for h in range(H):
            score = score + w_rows[h] * jnp.maximum(dots[h], 0.0)
        adm = r0 + sub_pos <= qpos
        sc = jnp.where(adm, score, -jnp.inf)
        sc_ref[pl.ds(r0, Tk), :] = sc
        rmax = jnp.maximum(rmax, jnp.max(sc, axis=0, keepdims=True))
        rmin = jnp.minimum(rmin, jnp.min(jnp.where(adm, score, jnp.inf), axis=0, keepdims=True))
        ge0 = ge0 + fold(jnp.where(sc >= 0.0, 1.0, 0.0))
        gt0 = gt0 + fold(jnp.where(sc > 0.0, 1.0, 0.0))
        return rmax, rmin, ge0, gt0

    row = lambda v: jnp.full((1, Tq), v, F32)
    rmax, rmin, ge0, gt0 = lax.fori_loop(
        0, nkt, score_body, (row(-jnp.inf), row(jnp.inf), jnp.zeros((ACC, Tq), F32), jnp.zeros((ACC, Tq), F32)))
    cnt_ge0 = jnp.sum(ge0, axis=0, keepdims=True)
    cnt_gt0 = jnp.sum(gt0, axis=0, keepdims=True)

    def count(pred):
        def body(kt, acc):
            r0 = pl.multiple_of(kt * Tk, Tk)
            return acc + fold(pred(r0, sc_ref[pl.ds(r0, Tk), :]))
        acc = lax.fori_loop(0, nkt, body, jnp.zeros((ACC, Tq), F32))
        return jnp.sum(acc, axis=0, keepdims=True)

    def count_ge(v):
        return count(lambda r0, s: jnp.where(s >= v, 1.0, 0.0))

    def count_gt(v):
        return count(lambda r0, s: jnp.where(s > v, 1.0, 0.0))

    kf = float(topk)
    log_k = math.log(kf)
    short = qpos < topk
    zero_tie = jnp.logical_and(cnt_gt0 < kf, cnt_ge0 >= kf)
    positive = cnt_gt0 >= kf
    tiny = row(FLT_MIN_NORMAL)
    lo_k = jnp.where(positive, _f32_to_key(tiny), _f32_to_key(rmin))
    hi_k = jnp.where(positive, _f32_to_key(rmax) + 1, _f32_to_key(-tiny))
    c_lo = jnp.where(positive, cnt_gt0, (qpos + 1).astype(F32))
    c_hi = jnp.where(positive, 0.0, cnt_ge0)
    done0 = jnp.where(jnp.logical_or(jnp.logical_or(short, zero_tie), c_lo == kf), 1.0, 0.0)

    def probe(st):
        it, lo_k, hi_k, c_lo, c_hi, w_lo, w_hi, last, done = st
        lo = _key_to_f32(lo_k)
        hi = _key_to_f32(hi_k - 1)
        a = (jnp.log(c_lo + 0.5) - log_k) * w_lo
        b = (log_k - jnp.log(c_hi + 0.5)) * w_hi
        frac = jnp.clip(a / jnp.maximum(a + b, 1e-30), 0.02, 0.98)
        cand_k = _f32_to_key(lo + frac * (hi - lo))
        mid_k = lo_k + lax.shift_right_logical(hi_k - lo_k, 1)
        cand_k = jnp.where(jnp.logical_and(it >= SEARCH_FAST_PASSES, it % 4 == 3), mid_k, cand_k)
        cand_k = jnp.minimum(jnp.maximum(cand_k, lo_k + 1), hi_k - 1)
        c = count_ge(_key_to_f32(cand_k))
        live = done == 0.0
        up = jnp.logical_and(live, c >= kf)
        dn = jnp.logical_and(live, c < kf)
        w_hi = jnp.where(jnp.logical_and(up, last > 0.0), w_hi * 0.5, jnp.where(dn, 1.0, w_hi))
        w_lo = jnp.where(jnp.logical_and(dn, last < 0.0), w_lo * 0.5, jnp.where(up, 1.0, w_lo))
        last = jnp.where(up, 1.0, jnp.where(dn, -1.0, last))
        lo_k = jnp.where(up, cand_k, lo_k)
        c_lo = jnp.where(up, c, c_lo)
        hi_k = jnp.where(dn, cand_k, hi_k)
        c_hi = jnp.where(dn, c, c_hi)
        done = jnp.where(jnp.logical_or(c_lo == kf, hi_k - lo_k <= 1), 1.0, done)
        return it + 1, lo_k, hi_k, c_lo, c_hi, w_lo, w_hi, last, done

    def search_cond(st):
        return jnp.logical_and(st[0] < SEARCH_MAX_PASSES, jnp.min(st[-1]) == 0.0)

    st = (jnp.int32(0), lo_k, hi_k, c_lo, c_hi, row(1.0), row(1.0), row(0.0), done0)
    st = lax.while_loop(search_cond, lambda st: probe(probe(st)), st)
    _, lo_k, hi_k, c_lo, c_hi = st[:5]

    lo_f = jnp.where(short, -jnp.inf, jnp.where(zero_tie, 0.0, _key_to_f32(lo_k)))
    hi_f = jnp.where(zero_tie, tiny, _key_to_f32(hi_k))
    c_lo = jnp.where(zero_tie, cnt_ge0, c_lo)
    c_hi = jnp.where(zero_tie, cnt_gt0, c_hi)
    cnt_gt = count_gt(lo_f)

    def unresolved(lo_f, hi_f, c_lo, c_hi, cnt_gt):
        inside = c_lo - c_hi
        u = jnp.logical_and(kf - c_hi < inside, c_lo - cnt_gt < inside)
        return jnp.where(short, 0.0, jnp.where(u, 1.0, 0.0))

    def refine_cond(st):
        return jnp.logical_and(st[0] < SEARCH_MAX_PASSES, jnp.max(unresolved(*st[1:])) > 0.0)

    def refine(st):
        it, lo_f, hi_f, c_lo, c_hi, cnt_gt = st
        live = unresolved(lo_f, hi_f, c_lo, c_hi, cnt_gt) > 0.0
        mid = 0.5 * lo_f + 0.5 * hi_f
        c = count_ge(mid)
        up = jnp.logical_and(live, c >= kf)
        dn = jnp.logical_and(live, c < kf)
        lo_f = jnp.where(up, mid, lo_f)
        c_lo = jnp.where(up, c, c_lo)
        hi_f = jnp.where(dn, mid, hi_f)
        c_hi = jnp.where(dn, c, c_hi)
        return it + 1, lo_f, hi_f, c_lo, c_hi, count_gt(lo_f)

    _, thr, _, c_thr, _, cnt_gt = lax.while_loop(
        refine_cond, refine, (jnp.int32(0), lo_f, hi_f, c_lo, c_hi, cnt_gt))

    quota = jnp.where(short, 0.0, kf - cnt_gt)
    before = jnp.where(lax.broadcasted_iota(I32, (Tk, Tk), 0) > lax.broadcasted_iota(I32, (Tk, Tk), 1),
                       1.0, 0.0).astype(BF16)

    def mask_body(kt, seen):
        r0 = pl.multiple_of(kt * Tk, Tk)
        sc = sc_ref[pl.ds(r0, Tk), :]
        eq = jnp.where(sc == thr, 1.0, 0.0)
        rank = _dot(before, eq.astype(BF16)) + seen
        tie = jnp.where(rank < quota, 0.0, -jnp.inf)
        sc_ref[pl.ds(r0, Tk), :] = jnp.where(sc > thr, 0.0, jnp.where(sc == thr, tie, -jnp.inf))
        return seen + jnp.sum(eq, axis=0, keepdims=True)

    lax.fori_loop(0, nkt, mask_body, jnp.zeros((1, Tq), F32))

    q_all = qb_ref[...].astype(F32)
    q_norm = jnp.sqrt(_dot_nt(head_sel, (q_all * q_all).astype(BF16)))
    shift = [q_norm[h:h + 1, :] * (ATT_BOUND_MARGIN * knorm_ref[h:h + 1, 0:1]) for h in range(H)]

    def att_fast(kt, l_all):
        r0 = pl.multiple_of(kt * Tk, Tk)
        bias = sc_ref[pl.ds(r0, Tk), :]
        for h in range(H):
            sl = slice(h * HD, (h + 1) * HD)
            s_ref[h] = _dot_nt(kb_ref[pl.ds(r0, Tk), sl], qb_ref[:, sl]) + bias - shift[h]
        l_rows = []
        for h in range(H):
            p = jnp.exp2(s_ref[h])
            l_rows.append(l_all[h:h + 1, :] + jnp.sum(p, axis=0, keepdims=True))
            acc_ref[h] += _dot(vT_ref[kt, h * HD:(h + 1) * HD, :], p.astype(BF16))
        return jnp.concatenate(l_rows, axis=0)

    acc_ref[...] = jnp.zeros_like(acc_ref)
    l_fast = lax.fori_loop(0, nkt, att_fast, jnp.zeros((H, Tq), F32))

    def att_slow(kt, carry):
        m_all, l_all = carry
        r0 = pl.multiple_of(kt * Tk, Tk)
        bias = sc_ref[pl.ds(r0, Tk), :]
        tile_max = []
        for h in range(H):
            sl = slice(h * HD, (h + 1) * HD)
            s = _dot_nt(kb_ref[pl.ds(r0, Tk), sl], qb_ref[:, sl]) + bias
            s_ref[h] = s
            tile_max.append(jnp.max(s, axis=0, keepdims=True))
        m_rows, l_rows = [], []
        for h in range(H):
            s = s_ref[h]
            m_old = m_all[h:h + 1, :]
            m_new = jnp.maximum(m_old, tile_max[h])
            alpha = jnp.exp2(m_old - m_new)
            p = jnp.exp2(s - m_new)
            l_rows.append(alpha * l_all[h:h + 1, :] + jnp.sum(p, axis=0, keepdims=True))
            acc_ref[h] = alpha * acc_ref[h] + _dot(vT_ref[kt, h * HD:(h + 1) * HD, :], p.astype(BF16))
            m_rows.append(m_new)
        return jnp.concatenate(m_rows, axis=0), jnp.concatenate(l_rows, axis=0)

    def redo():
        acc_ref[...] = jnp.zeros_like(acc_ref)
        init = (jnp.full((H, Tq), -1e30, F32), jnp.zeros((H, Tq), F32))
        return lax.fori_loop(0, nkt, att_slow, init)[1]

    l_fin = lax.cond(jnp.min(l_fast) > ATT_MIN_DENOM, lambda: l_fast, redo)
    out_t = jnp.concatenate([acc_ref[h] / l_fin[h:h + 1, :] for h in range(H)], axis=0)
    o_ref[...] = out_t.T.astype(BF16)


def _dsa(qb, kb, vb, qi, small, *, batch, seq, Tq, Tk):
    T = qb.shape[0]
    W = GROUP_W
    nq = seq // Tq
    qrow = lambda b, i: (b * nq + i, 0)
    full = lambda b, i: (b, 0)
    topk = min(TOPK_MAX, seq // 4)
    return pl.pallas_call(
        functools.partial(_dsa_kernel, Tq=Tq, Tk=Tk, seq=seq, topk=topk),
        grid=(batch, nq),
        in_specs=[
            pl.BlockSpec((Tq, W), qrow),
            pl.BlockSpec((Tq, W), qrow),
            pl.BlockSpec((Tq, LANES), qrow),
            pl.BlockSpec((seq, W), full),
            pl.BlockSpec((seq, W), full),
            pl.BlockSpec((seq, LANES), full),
        ],
        out_specs=pl.BlockSpec((Tq, W), qrow),
        out_shape=jax.ShapeDtypeStruct((T, W), BF16),
        scratch_shapes=[
            pltpu.VMEM((seq, Tq), F32),
            pltpu.VMEM((seq // Tk, W, Tk), BF16),
            pltpu.VMEM((seq, LANES), BF16),
            pltpu.VMEM((N_HEADS, Tk, Tq), F32),
            pltpu.VMEM((N_HEADS, HEAD_DIM, Tq), F32),
            pltpu.VMEM((N_HEADS, LANES), F32),
        ],
        compiler_params=pltpu.CompilerParams(
            dimension_semantics=("parallel", "arbitrary"), vmem_limit_bytes=VMEM_LIMIT),
        name="dsa",
    )(qb, qi, small, kb, vb, small)


def _merge_kernel(h_ref, oa_ref, ob_ref, g_ref, wa_ref, wb_ref, wo_ref, gpost_ref, o_ref):
    D = h_ref.shape[1]
    ya = _dot(oa_ref[...], wa_ref[...])
    yb = _dot(ob_ref[...], wb_ref[...])
    g = g_ref[...]
    merged = _sigmoid(g[:, :D]) * ya + _sigmoid(g[:, D:]) * yb
    mix = _dot(merged.astype(BF16), wo_ref[...])
    o_ref[...] = h_ref[...] + _rms(mix) * gpost_ref[...]


def _merge(h, oa, ob, g, wa, wb, wo, gpost, *, tm):
    T, D = h.shape
    W = GROUP_W
    row = lambda i: (i, 0)
    const = lambda i: (0, 0)
    return pl.pallas_call(
        _merge_kernel,
        grid=(T // tm,),
        in_specs=[
            pl.BlockSpec((tm, D), row),
            pl.BlockSpec((tm, W), row),
            pl.BlockSpec((tm, W), row),
            pl.BlockSpec((tm, 2 * D), row),
            pl.BlockSpec((W, D), const),
            pl.BlockSpec((W, D), const),
            pl.BlockSpec((D, D), const),
            pl.BlockSpec((1, D), const),
        ],
        out_specs=pl.BlockSpec((tm, D), row),
        out_shape=jax.ShapeDtypeStruct((T, D), F32),
        compiler_params=pltpu.CompilerParams(
            dimension_semantics=("parallel",), vmem_limit_bytes=VMEM_LIMIT),
        name="merge",
    )(h, oa, ob, g, wa, wb, wo, gpost)


def _ple_kernel(h_ref, p_ref, gpre_ref, wg_ref, wp_ref, gpost_ref, o_ref):
    h = h_ref[...]
    gate = _sigmoid(_dot((_rms(h) * gpre_ref[...]).astype(BF16), wg_ref[...]))
    e = _dot(p_ref[...].astype(BF16), wp_ref[...])
    o_ref[...] = h + _rms(gate * e) * gpost_ref[...]


def _ple(h, p, gpre, wg, wp, gpost, *, tm):
    T, D = h.shape
    P = p.shape[1]
    row = lambda i: (i, 0)
    const = lambda i: (0, 0)
    return pl.pallas_call(
        _ple_kernel,
        grid=(T // tm,),
        in_specs=[
            pl.BlockSpec((tm, D), row),
            pl.BlockSpec((tm, P), row),
            pl.BlockSpec((1, D), const),
            pl.BlockSpec((D, D), const),
            pl.BlockSpec((P, D), const),
            pl.BlockSpec((1, D), const),
        ],
        out_specs=pl.BlockSpec((tm, D), row),
        out_shape=jax.ShapeDtypeStruct((T, D), F32),
        compiler_params=pltpu.CompilerParams(
            dimension_semantics=("parallel",), vmem_limit_bytes=VMEM_LIMIT),
        name="ple",
    )(h, p, gpre, wg, wp, gpost)


def _rope_tables(seq):
    half = ROT_DIM // 2
    pos = jnp.arange(seq, dtype=F32)
    inv = ROPE_THETA ** (-jnp.arange(0, ROT_DIM, 2, dtype=F32) / ROT_DIM)
    ang = pos[:, None] * inv[None, :]
    cos, sin = jnp.cos(ang), jnp.sin(ang)
    ones = jnp.ones((seq, HEAD_DIM - ROT_DIM), F32)
    zeros_h = jnp.zeros((seq, half), F32)
    zeros_r = jnp.zeros((seq, HEAD_DIM - ROT_DIM), F32)
    c = jnp.concatenate([cos, cos, ones], axis=1)
    sa = jnp.concatenate([-sin, zeros_h, zeros_r], axis=1)
    sb = jnp.concatenate([zeros_h, sin, zeros_r], axis=1)
    rep = LANES // HEAD_DIM
    return jnp.tile(c, (1, rep)), jnp.tile(sa, (1, rep)), jnp.tile(sb, (1, rep))


def _pad_lanes(vec, offset):
    out = jnp.zeros((1, LANES), F32)
    return lax.dynamic_update_slice(out, vec.reshape(1, -1).astype(F32), (0, offset))


def kernel(x, p, ffn1_norm_pre, ffn1_norm_post, ffn1_w_in, ffn1_w_out, mix_norm_pre, mix_norm_post, mix_w_in, conv_w, a_log, dt_bias, dn_norm_g, idx_k_norm_g, w_br_a, w_br_b, mix_w_out, ffn2_norm_pre, ffn2_norm_post, ffn2_w_in, ffn2_w_out, ple_norm_pre, ple_norm_post, ple_w_gate, ple_w_proj):
    B, S, D = x.shape
    T = B * S
    W = GROUP_W
    depth = ffn1_w_in.shape[0]
    ff = ffn1_w_out.shape[1]

    tm_ffn = min(512, T)
    tf = ff // 2 if (ff // 2) % LANES == 0 else ff
    tm_mix = min(256, S)
    gdn_rows = min(256, S)
    gdn_chunk = min(64, gdn_rows)
    Tq = min(512, S)
    Tk = min(512, S)
    tm_post = min(512, T)

    rope_c, rope_sa, rope_sb = _rope_tables(S)
    ri = jnp.arange(W)
    bd = (ri[:, None] // HEAD_DIM == ri[None, :] // HEAD_DIM).astype(BF16)
    smscale = _pad_lanes(jnp.full((N_HEADS,), N_HEADS ** -0.5, F32), SM_WI)
    smscale = jnp.where(smscale == 0.0, 1.0, smscale)
    vec = lambda a: a.reshape(1, -1).astype(F32)

    h = x.reshape(T, D)
    for l in range(depth):
        h = _ffn(h, vec(ffn1_norm_pre[l]), ffn1_w_in[l].astype(BF16), ffn1_w_out[l].astype(BF16),
                 vec(ffn1_norm_post[l]), tm=tm_ffn, tf=tf)

        wi = mix_w_in[l]
        o_a, o_b = 4 * W, 4 * W + 2 * N_HEADS
        o_ki = o_b + 4 * W
        o_wi = o_ki + HEAD_DIM
        o_g = o_wi + N_HEADS
        w_cat = jnp.concatenate([
            wi[:, :4 * W], wi[:, o_b:o_ki], wi[:, o_g:o_g + 2 * D],
            wi[:, o_ki:o_g], wi[:, o_a:o_b],
            jnp.zeros((D, LANES - HEAD_DIM - 3 * N_HEADS), wi.dtype)], axis=1).astype(BF16)
        qkva, z, qb, kb, vb, qi, g, small = _mix_in(
            h, vec(mix_norm_pre[l]), w_cat, rope_c, rope_sa, rope_sb,
            _pad_lanes(idx_k_norm_g[l], SM_KI), smscale, tm=tm_mix, seq=S)

        o_gdn = _gdn(qkva, z, small, conv_w[l].astype(F32),
                     _pad_lanes(a_log[l], SM_A), _pad_lanes(dt_bias[l], SM_A),
                     a_log[l].reshape(N_HEADS, 1).astype(F32), dt_bias[l].reshape(N_HEADS, 1).astype(F32),
                     jnp.tile(vec(dn_norm_g[l]), (1, N_HEADS)), bd,
                     batch=B, seq=S, P=gdn_rows, C=gdn_chunk)
        o_dsa = _dsa(qb, kb, vb, qi, small, batch=B, seq=S, Tq=Tq, Tk=Tk)

        h = _merge(h, o_gdn, o_dsa, g, w_br_a[l].astype(BF16), w_br_b[l].astype(BF16),
                   mix_w_out[l].astype(BF16), vec(mix_norm_post[l]), tm=tm_post)
        h = _ffn(h, vec(ffn2_norm_pre[l]), ffn2_w_in[l].astype(BF16), ffn2_w_out[l].astype(BF16),
                 vec(ffn2_norm_post[l]), tm=tm_ffn, tf=tf)
        h = _ple(h, p[l].reshape(T, -1), vec(ple_norm_pre[l]), ple_w_gate[l].astype(BF16),
                 ple_w_proj[l].astype(BF16), vec(ple_norm_post[l]), tm=tm_post)
    return h.reshape(B, S, D)
```

```python
import functools
import math

import jax
import jax.numpy as jnp
from jax import lax
from jax.experimental import pallas as pl
from jax.experimental.pallas import tpu as pltpu

F32 = jnp.float32
BF16 = jnp.bfloat16
I32 = jnp.int32

EPS = 1e-6
N_HEADS = 8
HEAD_DIM = 64
GROUP_W = N_HEADS * HEAD_DIM
CONV_TAPS = 4
ROT_DIM = HEAD_DIM // 4
ROPE_THETA = 500000.0
TOPK_MAX = 256
LANES = 128
SUBLANES = 8
FLT_MIN_NORMAL = 1.1754943508222875e-38
LOG2_E = 1.4426950408889634
SEARCH_FAST_PASSES = 24
SEARCH_MAX_PASSES = 256
COUNT_ROWS = 128
ATT_BOUND_MARGIN = 1.02
ATT_MIN_DENOM = 1e-30
VMEM_LIMIT = 56 * 1024 * 1024

SM_KI = 0
SM_WI = 64
SM_A = 72
SM_B = 80


def _rms(x):
    return x * lax.rsqrt(jnp.mean(x * x, axis=-1, keepdims=True) + EPS)


def _dot(a, b):
    return jnp.dot(a, b, preferred_element_type=F32)


def _dot_nt(a, b):
    return lax.dot_general(a, b, (((1,), (1,)), ((), ())), preferred_element_type=F32)


def _split2(x):
    hi = x.astype(BF16)
    lo = (x - hi.astype(F32)).astype(BF16)
    return hi, lo


def _split3(x):
    hi = x.astype(BF16)
    r = x - hi.astype(F32)
    mid = r.astype(BF16)
    lo = (r - mid.astype(F32)).astype(BF16)
    return hi, mid, lo


def _softplus(x):
    return jnp.maximum(x, 0.0) + jnp.log(1.0 + jnp.exp(-jnp.abs(x)))


def _sigmoid(x):
    return 1.0 / (1.0 + jnp.exp(-x))


def _ffn_kernel(x_ref, gpre_ref, wg_ref, wu_ref, wo_ref, gpost_ref, o_ref, xn_ref, acc_ref, *, nj):
    j = pl.program_id(1)

    @pl.when(j == 0)
    def _():
        xn_ref[...] = (_rms(x_ref[...]) * gpre_ref[...]).astype(BF16)
        acc_ref[...] = jnp.zeros_like(acc_ref)

    xn = xn_ref[...]
    g = _dot(xn, wg_ref[...])
    u = _dot(xn, wu_ref[...])
    a = (g * _sigmoid(g) * u).astype(BF16)
    acc_ref[...] += _dot(a, wo_ref[...])

    @pl.when(j == nj - 1)
    def _():
        o_ref[...] = x_ref[...] + 0.5 * (_rms(acc_ref[...]) * gpost_ref[...])


def _ffn(h, gpre, w_in, w_out, gpost, *, tm, tf):
    T, D = h.shape
    FF = w_out.shape[0]
    nj = FF // tf
    return pl.pallas_call(
        functools.partial(_ffn_kernel, nj=nj),
        grid=(T // tm, nj),
        in_specs=[
            pl.BlockSpec((tm, D), lambda i, j: (i, 0)),
            pl.BlockSpec((1, D), lambda i, j: (0, 0)),
            pl.BlockSpec((D, tf), lambda i, j: (0, j)),
            pl.BlockSpec((D, tf), lambda i, j: (0, j + nj)),
            pl.BlockSpec((tf, D), lambda i, j: (j, 0)),
            pl.BlockSpec((1, D), lambda i, j: (0, 0)),
        ],
        out_specs=pl.BlockSpec((tm, D), lambda i, j: (i, 0)),
        out_shape=jax.ShapeDtypeStruct((T, D), F32),
        scratch_shapes=[pltpu.VMEM((tm, D), BF16), pltpu.VMEM((tm, D), F32)],
        compiler_params=pltpu.CompilerParams(
            dimension_semantics=("parallel", "arbitrary"), vmem_limit_bytes=VMEM_LIMIT),
        name="ffn",
    )(h, gpre, w_in, w_in, w_out, gpost)


def _rope(x, c, sa, sb):
    half = ROT_DIM // 2
    return x * c + pltpu.roll(x, LANES - half, 1) * sa + pltpu.roll(x, half, 1) * sb


def _mix_in_kernel(h_ref, gpre_ref, w_ref, c_ref, sa_ref, sb_ref, kig_ref, smscale_ref,
                   qkva_ref, z_ref, qb_ref, kb_ref, vb_ref, qi_ref, g_ref, small_ref):
    xn = (_rms(h_ref[...]) * gpre_ref[...]).astype(BF16)
    c, sa, sb = c_ref[...], sa_ref[...], sb_ref[...]
    W = GROUP_W

    def proj(lo, hi):
        return _dot(xn, w_ref[:, lo:hi])

    qkva_ref[...] = proj(0, 3 * W)
    z_ref[...] = proj(3 * W, 4 * W)

    def roped(lo, scale):
        y = proj(lo, lo + W)
        parts = [_rope(y[:, t * LANES:(t + 1) * LANES], c, sa, sb) for t in range(W // LANES)]
        return (jnp.concatenate(parts, axis=1) * scale).astype(BF16)

    inv_sqrt_d = HEAD_DIM ** -0.5
    qb_ref[...] = roped(4 * W, inv_sqrt_d * LOG2_E)
    kb_ref[...] = roped(5 * W, 1.0)
    vb_ref[...] = proj(6 * W, 7 * W).astype(BF16)
    qi_ref[...] = roped(7 * W, inv_sqrt_d)
    g_ref[...] = proj(8 * W, 12 * W)

    sm = proj(12 * W, 12 * W + LANES)
    lane = lax.broadcasted_iota(I32, sm.shape, 1)
    is_ki = lane < HEAD_DIM
    ms = jnp.sum(jnp.where(is_ki, sm * sm, 0.0), axis=-1, keepdims=True) * (1.0 / HEAD_DIM)
    ki = _rope(sm * lax.rsqrt(ms + EPS) * kig_ref[...], c, sa, sb)
    small_ref[...] = jnp.where(is_ki, ki, sm * smscale_ref[...])


def _mix_in(h, gpre, w, rope_c, rope_sa, rope_sb, kig, smscale, *, tm, seq):
    T, D = h.shape
    W = GROUP_W
    ncol = w.shape[1]
    nseq = seq // tm
    row = lambda i: (i, 0)
    const = lambda i: (0, 0)
    pos = lambda i: (i % nseq, 0)
    outs = [(3 * W, F32), (W, F32), (W, BF16), (W, BF16), (W, BF16), (W, BF16), (4 * W, F32), (LANES, F32)]
    return pl.pallas_call(
        _mix_in_kernel,
        grid=(T // tm,),
        in_specs=[
            pl.BlockSpec((tm, D), row),
            pl.BlockSpec((1, D), const),
            pl.BlockSpec((D, ncol), const),
            pl.BlockSpec((tm, LANES), pos),
            pl.BlockSpec((tm, LANES), pos),
            pl.BlockSpec((tm, LANES), pos),
            pl.BlockSpec((1, LANES), const),
            pl.BlockSpec((1, LANES), const),
        ],
        out_specs=[pl.BlockSpec((tm, n), row) for n, _ in outs],
        out_shape=[jax.ShapeDtypeStruct((T, n), dt) for n, dt in outs],
        compiler_params=pltpu.CompilerParams(
            dimension_semantics=("parallel",), vmem_limit_bytes=VMEM_LIMIT),
        name="mix_in",
    )(h, gpre, w, rope_c, rope_sa, rope_sb, kig, smscale)


def _gdn_kernel(qkv_ref, z_ref, small_ref, convw_ref, alog_ref, dtb_ref, alogc_ref, dtbc_ref,
                dng_ref, bd_ref, o_ref,
                xs_ref, q3_ref, k3_ref, v3_ref, pw_ref, qkd_ref, sol_ref, u_ref, wd_ref,
                qd_ref, kdT_ref, oc_ref, state_ref, *, P, C):
    W = GROUP_W
    HD = HEAD_DIM
    H = N_HEADS
    shift = int(math.log2(C))
    i = pl.program_id(1)

    @pl.when(i == 0)
    def _():
        xs_ref[0:8, :] = jnp.zeros((8, 3 * W), F32)
        state_ref[...] = jnp.zeros_like(state_ref)

    xs_ref[8:8 + P, :] = qkv_ref[...]
    w = convw_ref[...]
    y = w[CONV_TAPS - 1:CONV_TAPS, :] * xs_ref[8:8 + P, :]
    for tap in range(CONV_TAPS - 1):
        off = 8 - (CONV_TAPS - 1) + tap
        y = y + w[tap:tap + 1, :] * xs_ref[off:off + P, :]
    xs_ref[0:8, :] = xs_ref[P:P + 8, :]
    y = y * _sigmoid(y)
    q, k, v = y[:, :W], y[:, W:2 * W], y[:, 2 * W:]

    bd = bd_ref[...]

    def head_sum(x):
        return _dot(x.astype(BF16), bd)

    qn = q * lax.rsqrt(head_sum(q * q) + EPS) * (HD ** -0.5)
    kn = k * lax.rsqrt(head_sum(k * k) + EPS)
    for h in range(H):
        sl = slice(h * HD, (h + 1) * HD)
        q3_ref[h] = qn[:, sl]
        k3_ref[h] = kn[:, sl]
        v3_ref[h] = v[:, sl]

    sm = small_ref[...]
    gfull = -jnp.exp(alog_ref[...]) * _softplus(sm + dtb_ref[...])
    bfull = _sigmoid(sm)
    gT = -jnp.exp(alogc_ref[...]) * _softplus(sm.T[SM_A:SM_A + H, :] + dtbc_ref[...])

    ri = lax.broadcasted_iota(I32, (P, P), 0)
    ci = lax.broadcasted_iota(I32, (P, P), 1)
    same = lax.shift_right_logical(ri, shift) == lax.shift_right_logical(ci, shift)
    incl = jnp.logical_and(same, ri >= ci)
    diag = ri == ci
    tri = jnp.where(incl, 1.0, 0.0).astype(BF16)
    triT = jnp.where(same, jnp.where(ci >= ri, 1.0, 0.0), 0.0).astype(BF16)
    blk = jnp.where(same, 1.0, 0.0).astype(BF16)

    g_parts = _split3(gfull)
    gc = sum(_dot(tri, part) for part in g_parts)
    gtot = sum(_dot(blk, part) for part in g_parts)
    gcT = sum(_dot(part, triT) for part in _split3(gT))
    egc = jnp.exp(gc)
    ekd = jnp.exp(gtot - gc)

    for h in range(H):
        K, Q, V = k3_ref[h], q3_ref[h], v3_ref[h]
        Kb = K.astype(BF16)
        gcol = gc[:, SM_A + h:SM_A + h + 1]
        grow = gcT[h:h + 1, :]
        bcol = bfull[:, SM_B + h:SM_B + h + 1]
        ecol = egc[:, SM_A + h:SM_A + h + 1]
        decay = jnp.where(incl, jnp.exp(gcol - grow), 0.0)
        n_mat = jnp.where(diag, 0.0, -(bcol * _dot_nt(Kb, Kb) * decay)).astype(BF16)
        pw_ref[0, h] = n_mat
        qkd_ref[h] = (_dot_nt(Q.astype(BF16), Kb) * decay).astype(BF16)
        rhs = jnp.concatenate([V * bcol, K * (bcol * ecol)], axis=1)
        sol_ref[h] = rhs + _dot(n_mat, rhs.astype(BF16))
        qd_ref[h] = (Q * ecol).astype(BF16)
        kdT_ref[h] = (K * ekd[:, SM_A + h:SM_A + h + 1]).T.astype(BF16)

    for it in range(shift - 1):
        src, dst = it % 2, 1 - it % 2
        for h in range(H):
            pw = pw_ref[src, h]
            pw_ref[dst, h] = _dot(pw, pw).astype(BF16)
        for h in range(H):
            y = sol_ref[h]
            sol_ref[h] = y + _dot(pw_ref[dst, h], y.astype(BF16))
    for h in range(H):
        sol = sol_ref[h]
        u_ref[h] = sol[:, :HD]
        wd_ref[h] = sol[:, HD:].astype(BF16)

    for c in range(P // C):
        rows = slice(c * C, (c + 1) * C)
        states = [state_ref[h] for h in range(H)]
        sbf = [s.astype(BF16) for s in states]
        v_new = [(u_ref[h, rows, :] - _dot(wd_ref[h, rows, :], sbf[h])).astype(BF16) for h in range(H)]
        for h in range(H):
            oc_ref[rows, h * HD:(h + 1) * HD] = (
                _dot(qd_ref[h, rows, :], sbf[h]) + _dot(qkd_ref[h, rows, rows], v_new[h]))
        for h in range(H):
            g_tot = jnp.exp(gcT[h:h + 1, (c + 1) * C - 1:(c + 1) * C])
            state_ref[h] = states[h] * g_tot + _dot(kdT_ref[h, :, rows], v_new[h])

    o = oc_ref[...]
    ms = head_sum(o * o) * (1.0 / HD)
    zz = z_ref[...]
    o_ref[...] = (o * lax.rsqrt(ms + EPS) * dng_ref[...] * (zz * _sigmoid(zz))).astype(BF16)


def _gdn(qkva, z, small, convw, alog_row, dtb_row, alog_col, dtb_col, dng, bd, *, batch, seq, P, C):
    T = qkva.shape[0]
    W = GROUP_W
    H, HD = N_HEADS, HEAD_DIM
    nblk = seq // P
    row = lambda b, i: (b * nblk + i, 0)
    const = lambda b, i: (0, 0)
    return pl.pallas_call(
        functools.partial(_gdn_kernel, P=P, C=C),
        grid=(batch, nblk),
        in_specs=[
            pl.BlockSpec((P, 3 * W), row),
            pl.BlockSpec((P, W), row),
            pl.BlockSpec((P, LANES), row),
            pl.BlockSpec((CONV_TAPS, 3 * W), const),
            pl.BlockSpec((1, LANES), const),
            pl.BlockSpec((1, LANES), const),
            pl.BlockSpec((H, 1), const),
            pl.BlockSpec((H, 1), const),
            pl.BlockSpec((1, W), const),
            pl.BlockSpec((W, W), const),
        ],
        out_specs=pl.BlockSpec((P, W), row),
        out_shape=jax.ShapeDtypeStruct((T, W), BF16),
        scratch_shapes=[
            pltpu.VMEM((P + 8, 3 * W), F32),
            pltpu.VMEM((H, P, HD), F32),
            pltpu.VMEM((H, P, HD), F32),
            pltpu.VMEM((H, P, HD), F32),
            pltpu.VMEM((2, H, P, P), BF16),
            pltpu.VMEM((H, P, P), BF16),
            pltpu.VMEM((H, P, 2 * HD), F32),
            pltpu.VMEM((H, P, HD), F32),
            pltpu.VMEM((H, P, HD), BF16),
            pltpu.VMEM((H, P, HD), BF16),
            pltpu.VMEM((H, HD, P), BF16),
            pltpu.VMEM((P, W), F32),
            pltpu.VMEM((H, HD, HD), F32),
        ],
        compiler_params=pltpu.CompilerParams(
            dimension_semantics=("parallel", "arbitrary"), vmem_limit_bytes=VMEM_LIMIT),
        name="gdn",
    )(qkva, z, small, convw, alog_row, dtb_row, alog_col, dtb_col, dng, bd)


def _key_to_f32(key):
    bits = key ^ (lax.shift_right_arithmetic(key, 31) & 0x7FFFFFFF)
    return pltpu.bitcast(bits, F32)


def _f32_to_key(x):
    bits = pltpu.bitcast(x, I32)
    return bits ^ (lax.shift_right_arithmetic(bits, 31) & 0x7FFFFFFF)


def _dsa_kernel(qb_ref, qi_ref, wsm_ref, kb_ref, vb_ref, ksm_ref, o_ref,
                sc_ref, vT_ref, kib_ref, s_ref, acc_ref, knorm_ref, *, Tq, Tk, seq, topk):
    HD = HEAD_DIM
    H = N_HEADS
    W = GROUP_W
    ACC = 2 * SUBLANES
    i = pl.program_id(1)
    nkt = ((i + 1) * Tq + Tk - 1) // Tk
    qpos = i * Tq + lax.broadcasted_iota(I32, (1, Tq), 1)
    sub_pos = lax.broadcasted_iota(I32, (Tk, Tq), 0)
    head_rows = lax.broadcasted_iota(I32, (H, W), 0)
    head_of_lane = lax.shift_right_logical(lax.broadcasted_iota(I32, (H, W), 1), int(math.log2(HD)))
    head_sel = jnp.where(head_rows == head_of_lane, 1.0, 0.0).astype(BF16)

    @pl.when(i == 0)
    def _():
        k_norm2 = jnp.zeros((H, 1), F32)
        for kt in range(seq // Tk):
            rows = slice(kt * Tk, (kt + 1) * Tk)
            vT_ref[kt] = vb_ref[rows, :].astype(F32).T.astype(BF16)
            k_tile = kb_ref[rows, :].astype(F32)
            per_head = _dot_nt(head_sel, (k_tile * k_tile).astype(BF16))
            k_norm2 = jnp.maximum(k_norm2, jnp.max(per_head, axis=1, keepdims=True))
        knorm_ref[...] = jnp.broadcast_to(jnp.sqrt(k_norm2), knorm_ref.shape)
        kib_ref[...] = ksm_ref[...].astype(BF16)

    wT = wsm_ref[...].T
    qi_heads = [qi_ref[:, h * HD:(h + 1) * HD] for h in range(H)]
    w_rows = [wT[SM_WI + h:SM_WI + h + 1, :] for h in range(H)]

    def fold(m):
        return jnp.sum(m.reshape(m.shape[0] // ACC, ACC, Tq), axis=0)

    def score_body(kt, carry):
        rmax, rmin, ge0, gt0 = carry
        r0 = pl.multiple_of(kt * Tk, Tk)
        ki = kib_ref[pl.ds(r0, Tk), :][:, SM_KI:SM_KI + HD]
        dots = [_dot_nt(ki, qi_heads[h]) for h in range(H)]
        score = jnp.zeros((Tk, Tq), F32)
        for h in range(H):
            score = score + w_rows[h] * jnp.maximum(dots[h], 0.0)
        adm = r0 + sub_pos <= qpos
        sc = jnp.where(adm, score, -jnp.inf)
        sc_ref[pl.ds(r0, Tk), :] = sc
        rmax = jnp.maximum(rmax, jnp.max(sc, axis=0, keepdims=True))
        rmin = jnp.minimum(rmin, jnp.min(jnp.where(adm, score, jnp.inf), axis=0, keepdims=True))
        ge0 = ge0 + fold(jnp.where(sc >= 0.0, 1.0, 0.0))
        gt0 = gt0 + fold(jnp.where(sc > 0.0, 1.0, 0.0))
        return rmax, rmin, ge0, gt0

    row = lambda v: jnp.full((1, Tq), v, F32)
    rmax, rmin, ge0, gt0 = lax.fori_loop(
        0, nkt, score_body, (row(-jnp.inf), row(jnp.inf), jnp.zeros((ACC, Tq), F32), jnp.zeros((ACC, Tq), F32)))
    cnt_ge0 = jnp.sum(ge0, axis=0, keepdims=True)
    cnt_gt0 = jnp.sum(gt0, axis=0, keepdims=True)

    def count(pred):
        def body(j, acc):
            r0 = pl.multiple_of(j * COUNT_ROWS, COUNT_ROWS)
            return acc + fold(pred(r0, sc_ref[pl.ds(r0, COUNT_ROWS), :]))
        acc = lax.fori_loop(0, nkt * (Tk // COUNT_ROWS), body, jnp.zeros((ACC, Tq), F32))
        return jnp.sum(acc, axis=0, keepdims=True)

    def count_ge(v):
        return count(lambda r0, s: jnp.where(s >= v, 1.0, 0.0))

    def count_gt(v):
        return count(lambda r0, s: jnp.where(s > v, 1.0, 0.0))

    kf = float(topk)
    log_k = math.log(kf)
    short = qpos < topk
    zero_tie = jnp.logical_and(cnt_gt0 < kf, cnt_ge0 >= kf)
    positive = cnt_gt0 >= kf
    tiny = row(FLT_MIN_NORMAL)
    lo_k = jnp.where(positive, _f32_to_key(tiny), _f32_to_key(rmin))
    hi_k = jnp.where(positive, _f32_to_key(rmax) + 1, _f32_to_key(-tiny))
    c_lo = jnp.where(positive, cnt_gt0, (qpos + 1).astype(F32))
    c_hi = jnp.where(positive, 0.0, cnt_ge0)
    done0 = jnp.where(jnp.logical_or(jnp.logical_or(short, zero_tie), c_lo == kf), 1.0, 0.0)

    def probe(st):
        it, lo_k, hi_k, c_lo, c_hi, w_lo, w_hi, last, done = st
        lo = _key_to_f32(lo_k)
        hi = _key_to_f32(hi_k - 1)
        a = (jnp.log(c_lo + 0.5) - log_k) * w_lo
        b = (log_k - jnp.log(c_hi + 0.5)) * w_hi
        frac = jnp.clip(a / jnp.maximum(a + b, 1e-30), 0.02, 0.98)
        cand_k = _f32_to_key(lo + frac * (hi - lo))
        mid_k = lo_k + lax.shift_right_logical(hi_k - lo_k, 1)
        cand_k = jnp.where(jnp.logical_and(it >= SEARCH_FAST_PASSES, it % 4 == 3), mid_k, cand_k)
        cand_k = jnp.minimum(jnp.maximum(cand_k, lo_k + 1), hi_k - 1)
        c = count_ge(_key_to_f32(cand_k))
        live = done == 0.0
        up = jnp.logical_and(live, c >= kf)
        dn = jnp.logical_and(live, c < kf)
        w_hi = jnp.where(jnp.logical_and(up, last > 0.0), w_hi * 0.5, jnp.where(dn, 1.0, w_hi))
        w_lo = jnp.where(jnp.logical_and(dn, last < 0.0), w_lo * 0.5, jnp.where(up, 1.0, w_lo))
        last = jnp.where(up, 1.0, jnp.where(dn, -1.0, last))
        lo_k = jnp.where(up, cand_k, lo_k)
        c_lo = jnp.where(up, c, c_lo)
        hi_k = jnp.where(dn, cand_k, hi_k)
        c_hi = jnp.where(dn, c, c_hi)
        done = jnp.where(jnp.logical_or(c_lo == kf, hi_k - lo_k <= 1), 1.0, done)
        return it + 1, lo_k, hi_k, c_lo, c_hi, w_lo, w_hi, last, done

    def search_cond(st):
        far = jnp.where(jnp.logical_and(st[-1] == 0.0, kf - st[4] > 1.0), 1.0, 0.0)
        return jnp.logical_and(st[0] < SEARCH_MAX_PASSES, jnp.max(far) > 0.0)

    st = (jnp.int32(0), lo_k, hi_k, c_lo, c_hi, row(1.0), row(1.0), row(0.0), done0)
    st = lax.while_loop(search_cond, lambda st: probe(probe(st)), st)
    _, lo_k, hi_k, c_lo, c_hi = st[:5]
    open_rows = st[-1] == 0.0

    lo_f = jnp.where(short, -jnp.inf, jnp.where(zero_tie, 0.0, _key_to_f32(lo_k)))
    hi_f = jnp.where(zero_tie, tiny, _key_to_f32(hi_k))
    c_lo = jnp.where(zero_tie, cnt_ge0, c_lo)
    c_hi = jnp.where(zero_tie, cnt_gt0, c_hi)

    def below_body(kt, best):
        r0 = pl.multiple_of(kt * Tk, Tk)
        s = sc_ref[pl.ds(r0, Tk), :]
        cand = jnp.where(s < hi_f, s, -jnp.inf)
        return jnp.maximum(best, jnp.max(cand.reshape(Tk // ACC, ACC, Tq), axis=0))

    below = jnp.max(lax.fori_loop(0, nkt, below_body, jnp.full((ACC, Tq), -jnp.inf, F32)),
                    axis=0, keepdims=True)
    lo_f = jnp.where(open_rows, below, lo_f)
    c_lo = jnp.where(open_rows, c_hi + 1.0, c_lo)
    cnt_gt = count_gt(lo_f)

    def unresolved(lo_f, hi_f, c_lo, c_hi, cnt_gt):
        inside = c_lo - c_hi
        u = jnp.logical_and(kf - c_hi < inside, c_lo - cnt_gt < inside)
        return jnp.where(short, 0.0, jnp.where(u, 1.0, 0.0))

    def refine_cond(st):
        return jnp.logical_and(st[0] < SEARCH_MAX_PASSES, jnp.max(unresolved(*st[1:])) > 0.0)

    def refine(st):
        it, lo_f, hi_f, c_lo, c_hi, cnt_gt = st
        live = unresolved(lo_f, hi_f, c_lo, c_hi, cnt_gt) > 0.0
        mid = 0.5 * lo_f + 0.5 * hi_f
        c = count_ge(mid)
        up = jnp.logical_and(live, c >= kf)
        dn = jnp.logical_and(live, c < kf)
        lo_f = jnp.where(up, mid, lo_f)
        c_lo = jnp.where(up, c, c_lo)
        hi_f = jnp.where(dn, mid, hi_f)
        c_hi = jnp.where(dn, c, c_hi)
        return it + 1, lo_f, hi_f, c_lo, c_hi, count_gt(lo_f)

    _, thr, _, c_thr, _, cnt_gt = lax.while_loop(
        refine_cond, refine, (jnp.int32(0), lo_f, hi_f, c_lo, c_hi, cnt_gt))

    quota = jnp.where(short, 0.0, kf - cnt_gt)
    before = jnp.where(lax.broadcasted_iota(I32, (Tk, Tk), 0) > lax.broadcasted_iota(I32, (Tk, Tk), 1),
                       1.0, 0.0).astype(BF16)

    def mask_body(kt, seen):
        r0 = pl.multiple_of(kt * Tk, Tk)
        sc = sc_ref[pl.ds(r0, Tk), :]
        eq = jnp.where(sc == thr, 1.0, 0.0)
        rank = _dot(before, eq.astype(BF16)) + seen
        tie = jnp.where(rank < quota, 0.0, -jnp.inf)
        sc_ref[pl.ds(r0, Tk), :] = jnp.where(sc > thr, 0.0, jnp.where(sc == thr, tie, -jnp.inf))
        return seen + jnp.sum(eq, axis=0, keepdims=True)

    lax.fori_loop(0, nkt, mask_body, jnp.zeros((1, Tq), F32))

    q_all = qb_ref[...].astype(F32)
    q_norm = jnp.sqrt(_dot_nt(head_sel, (q_all * q_all).astype(BF16)))
    shift = [q_norm[h:h + 1, :] * (ATT_BOUND_MARGIN * knorm_ref[h:h + 1, 0:1]) for h in range(H)]

    def att_fast(kt, l_all):
        r0 = pl.multiple_of(kt * Tk, Tk)
        bias = sc_ref[pl.ds(r0, Tk), :]
        for h in range(H):
            sl = slice(h * HD, (h + 1) * HD)
            s_ref[h] = _dot_nt(kb_ref[pl.ds(r0, Tk), sl], qb_ref[:, sl]) + bias - shift[h]
        l_rows = []
        for h in range(H):
            p = jnp.exp2(s_ref[h])
            l_rows.append(l_all[h:h + 1, :] + jnp.sum(p, axis=0, keepdims=True))
            acc_ref[h] += _dot(vT_ref[kt, h * HD:(h + 1) * HD, :], p.astype(BF16))
        return jnp.concatenate(l_rows, axis=0)

    acc_ref[...] = jnp.zeros_like(acc_ref)
    l_fast = lax.fori_loop(0, nkt, att_fast, jnp.zeros((H, Tq), F32))

    def att_slow(kt, carry):
        m_all, l_all = carry
        r0 = pl.multiple_of(kt * Tk, Tk)
        bias = sc_ref[pl.ds(r0, Tk), :]
        tile_max = []
        for h in range(H):
            sl = slice(h * HD, (h + 1) * HD)
            s = _dot_nt(kb_ref[pl.ds(r0, Tk), sl], qb_ref[:, sl]) + bias
            s_ref[h] = s
            tile_max.append(jnp.max(s, axis=0, keepdims=True))
        m_rows, l_rows = [], []
        for h in range(H):
            s = s_ref[h]
            m_old = m_all[h:h + 1, :]
            m_new = jnp.maximum(m_old, tile_max[h])
            alpha = jnp.exp2(m_old - m_new)
            p = jnp.exp2(s - m_new)
            l_rows.append(alpha * l_all[h:h + 1, :] + jnp.sum(p, axis=0, keepdims=True))
            acc_ref[h] = alpha * acc_ref[h] + _dot(vT_ref[kt, h * HD:(h + 1) * HD, :], p.astype(BF16))
            m_rows.append(m_new)
        return jnp.concatenate(m_rows, axis=0), jnp.concatenate(l_rows, axis=0)

    def redo():
        acc_ref[...] = jnp.zeros_like(acc_ref)
        init = (jnp.full((H, Tq), -1e30, F32), jnp.zeros((H, Tq), F32))
        return lax.fori_loop(0, nkt, att_slow, init)[1]

    l_fin = lax.cond(jnp.min(l_fast) > ATT_MIN_DENOM, lambda: l_fast, redo)
    out_t = jnp.concatenate([acc_ref[h] / l_fin[h:h + 1, :] for h in range(H)], axis=0)
    o_ref[...] = out_t.T.astype(BF16)


def _dsa(qb, kb, vb, qi, small, *, batch, seq, Tq, Tk):
    T = qb.shape[0]
    W = GROUP_W
    nq = seq // Tq
    qrow = lambda b, i: (b * nq + i, 0)
    full = lambda b, i: (b, 0)
    topk = min(TOPK_MAX, seq // 4)
    return pl.pallas_call(
        functools.partial(_dsa_kernel, Tq=Tq, Tk=Tk, seq=seq, topk=topk),
        grid=(batch, nq),
        in_specs=[
            pl.BlockSpec((Tq, W), qrow),
            pl.BlockSpec((Tq, W), qrow),
            pl.BlockSpec((Tq, LANES), qrow),
            pl.BlockSpec((seq, W), full),
            pl.BlockSpec((seq, W), full),
            pl.BlockSpec((seq, LANES), full),
        ],
        out_specs=pl.BlockSpec((Tq, W), qrow),
        out_shape=jax.ShapeDtypeStruct((T, W), BF16),
        scratch_shapes=[
            pltpu.VMEM((seq, Tq), F32),
            pltpu.VMEM((seq // Tk, W, Tk), BF16),
            pltpu.VMEM((seq, LANES), BF16),
            pltpu.VMEM((N_HEADS, Tk, Tq), F32),
            pltpu.VMEM((N_HEADS, HEAD_DIM, Tq), F32),
            pltpu.VMEM((N_HEADS, LANES), F32),
        ],
        compiler_params=pltpu.CompilerParams(
            dimension_semantics=("parallel", "arbitrary"), vmem_limit_bytes=VMEM_LIMIT),
        name="dsa",
    )(qb, qi, small, kb, vb, small)


def _merge_kernel(h_ref, oa_ref, ob_ref, g_ref, wa_ref, wb_ref, wo_ref, gpost_ref, o_ref):
    D = h_ref.shape[1]
    ya = _dot(oa_ref[...], wa_ref[...])
    yb = _dot(ob_ref[...], wb_ref[...])
    g = g_ref[...]
    merged = _sigmoid(g[:, :D]) * ya + _sigmoid(g[:, D:]) * yb
    mix = _dot(merged.astype(BF16), wo_ref[...])
    o_ref[...] = h_ref[...] + _rms(mix) * gpost_ref[...]


def _merge(h, oa, ob, g, wa, wb, wo, gpost, *, tm):
    T, D = h.shape
    W = GROUP_W
    row = lambda i: (i, 0)
    const = lambda i: (0, 0)
    return pl.pallas_call(
        _merge_kernel,
        grid=(T // tm,),
        in_specs=[
            pl.BlockSpec((tm, D), row),
            pl.BlockSpec((tm, W), row),
            pl.BlockSpec((tm, W), row),
            pl.BlockSpec((tm, 2 * D), row),
            pl.BlockSpec((W, D), const),
            pl.BlockSpec((W, D), const),
            pl.BlockSpec((D, D), const),
            pl.BlockSpec((1, D), const),
        ],
        out_specs=pl.BlockSpec((tm, D), row),
        out_shape=jax.ShapeDtypeStruct((T, D), F32),
        compiler_params=pltpu.CompilerParams(
            dimension_semantics=("parallel",), vmem_limit_bytes=VMEM_LIMIT),
        name="merge",
    )(h, oa, ob, g, wa, wb, wo, gpost)


def _ple_kernel(h_ref, p_ref, gpre_ref, wg_ref, wp_ref, gpost_ref, o_ref):
    h = h_ref[...]
    gate = _sigmoid(_dot((_rms(h) * gpre_ref[...]).astype(BF16), wg_ref[...]))
    e = _dot(p_ref[...].astype(BF16), wp_ref[...])
    o_ref[...] = h + _rms(gate * e) * gpost_ref[...]


def _ple(h, p, gpre, wg, wp, gpost, *, tm):
    T, D = h.shape
    P = p.shape[1]
    row = lambda i: (i, 0)
    const = lambda i: (0, 0)
    return pl.pallas_call(
        _ple_kernel,
        grid=(T // tm,),
        in_specs=[
            pl.BlockSpec((tm, D), row),
            pl.BlockSpec((tm, P), row),
            pl.BlockSpec((1, D), const),
            pl.BlockSpec((D, D), const),
            pl.BlockSpec((P, D), const),
            pl.BlockSpec((1, D), const),
        ],
        out_specs=pl.BlockSpec((tm, D), row),
        out_shape=jax.ShapeDtypeStruct((T, D), F32),
        compiler_params=pltpu.CompilerParams(
            dimension_semantics=("parallel",), vmem_limit_bytes=VMEM_LIMIT),
        name="ple",
    )(h, p, gpre, wg, wp, gpost)


def _rope_tables(seq):
    half = ROT_DIM // 2
    pos = jnp.arange(seq, dtype=F32)
    inv = ROPE_THETA ** (-jnp.arange(0, ROT_DIM, 2, dtype=F32) / ROT_DIM)
    ang = pos[:, None] * inv[None, :]
    cos, sin = jnp.cos(ang), jnp.sin(ang)
    ones = jnp.ones((seq, HEAD_DIM - ROT_DIM), F32)
    zeros_h = jnp.zeros((seq, half), F32)
    zeros_r = jnp.zeros((seq, HEAD_DIM - ROT_DIM), F32)
    c = jnp.concatenate([cos, cos, ones], axis=1)
    sa = jnp.concatenate([-sin, zeros_h, zeros_r], axis=1)
    sb = jnp.concatenate([zeros_h, sin, zeros_r], axis=1)
    rep = LANES // HEAD_DIM
    return jnp.tile(c, (1, rep)), jnp.tile(sa, (1, rep)), jnp.tile(sb, (1, rep))


def _pad_lanes(vec, offset):
    out = jnp.zeros((1, LANES), F32)
    return lax.dynamic_update_slice(out, vec.reshape(1, -1).astype(F32), (0, offset))


def kernel(x, p, ffn1_norm_pre, ffn1_norm_post, ffn1_w_in, ffn1_w_out, mix_norm_pre, mix_norm_post, mix_w_in, conv_w, a_log, dt_bias, dn_norm_g, idx_k_norm_g, w_br_a, w_br_b, mix_w_out, ffn2_norm_pre, ffn2_norm_post, ffn2_w_in, ffn2_w_out, ple_norm_pre, ple_norm_post, ple_w_gate, ple_w_proj):
    B, S, D = x.shape
    T = B * S
    W = GROUP_W
    depth = ffn1_w_in.shape[0]
    ff = ffn1_w_out.shape[1]

    tm_ffn = min(512, T)
    tf = ff // 2 if (ff // 2) % LANES == 0 else ff
    tm_mix = min(256, S)
    gdn_rows = min(256, S)
    gdn_chunk = min(64, gdn_rows)
    Tq = min(512, S)
    Tk = min(512, S)
    tm_post = min(512, T)

    rope_c, rope_sa, rope_sb = _rope_tables(S)
    ri = jnp.arange(W)
    bd = (ri[:, None] // HEAD_DIM == ri[None, :] // HEAD_DIM).astype(BF16)
    smscale = _pad_lanes(jnp.full((N_HEADS,), N_HEADS ** -0.5, F32), SM_WI)
    smscale = jnp.where(smscale == 0.0, 1.0, smscale)
    vec = lambda a: a.reshape(1, -1).astype(F32)

    h = x.reshape(T, D)
    for l in range(depth):
        h = _ffn(h, vec(ffn1_norm_pre[l]), ffn1_w_in[l].astype(BF16), ffn1_w_out[l].astype(BF16),
                 vec(ffn1_norm_post[l]), tm=tm_ffn, tf=tf)

        wi = mix_w_in[l]
        o_a, o_b = 4 * W, 4 * W + 2 * N_HEADS
        o_ki = o_b + 4 * W
        o_wi = o_ki + HEAD_DIM
        o_g = o_wi + N_HEADS
        w_cat = jnp.concatenate([
            wi[:, :4 * W], wi[:, o_b:o_ki], wi[:, o_g:o_g + 2 * D],
            wi[:, o_ki:o_g], wi[:, o_a:o_b],
            jnp.zeros((D, LANES - HEAD_DIM - 3 * N_HEADS), wi.dtype)], axis=1).astype(BF16)
        qkva, z, qb, kb, vb, qi, g, small = _mix_in(
            h, vec(mix_norm_pre[l]), w_cat, rope_c, rope_sa, rope_sb,
            _pad_lanes(idx_k_norm_g[l], SM_KI), smscale, tm=tm_mix, seq=S)

        o_gdn = _gdn(qkva, z, small, conv_w[l].astype(F32),
                     _pad_lanes(a_log[l], SM_A), _pad_lanes(dt_bias[l], SM_A),
                     a_log[l].reshape(N_HEADS, 1).astype(F32), dt_bias[l].reshape(N_HEADS, 1).astype(F32),
                     jnp.tile(vec(dn_norm_g[l]), (1, N_HEADS)), bd,
                     batch=B, seq=S, P=gdn_rows, C=gdn_chunk)
        o_dsa = _dsa(qb, kb, vb, qi, small, batch=B, seq=S, Tq=Tq, Tk=Tk)

        h = _merge(h, o_gdn, o_dsa, g, w_br_a[l].astype(BF16), w_br_b[l].astype(BF16),
                   mix_w_out[l].astype(BF16), vec(mix_norm_post[l]), tm=tm_post)
        h = _ffn(h, vec(ffn2_norm_pre[l]), ffn2_w_in[l].astype(BF16), ffn2_w_out[l].astype(BF16),
                 vec(ffn2_norm_post[l]), tm=tm_ffn, tf=tf)
        h = _ple(h, p[l].reshape(T, -1), vec(ple_norm_pre[l]), ple_w_gate[l].astype(BF16),
                 ple_w_proj[l].astype(BF16), vec(ple_norm_post[l]), tm=tm_post)
    return h.reshape(B, S, D)
```

```python
import functools
import math

import jax
import jax.numpy as jnp
from jax import lax
from jax.experimental import pallas as pl
from jax.experimental.pallas import tpu as pltpu

F32 = jnp.float32
BF16 = jnp.bfloat16
I32 = jnp.int32

EPS = 1e-6
N_HEADS = 8
HEAD_DIM = 64
GROUP_W = N_HEADS * HEAD_DIM
CONV_TAPS = 4
ROT_DIM = HEAD_DIM // 4
ROPE_THETA = 500000.0
TOPK_MAX = 256
LANES = 128
SUBLANES = 8
FLT_MIN_NORMAL = 1.1754943508222875e-38
LOG2_E = 1.4426950408889634
SEARCH_FAST_PASSES = 24
SEARCH_MAX_PASSES = 256
COUNT_ROWS = 128
ATT_BOUND_MARGIN = 1.02
ATT_MIN_DENOM = 1e-30
VMEM_LIMIT = 56 * 1024 * 1024

SM_KI = 0
SM_WI = 64
SM_A = 72
SM_B = 80


def _rms(x):
    return x * lax.rsqrt(jnp.mean(x * x, axis=-1, keepdims=True) + EPS)


def _dot(a, b):
    return jnp.dot(a, b, preferred_element_type=F32)


def _dot_nt(a, b):
    return lax.dot_general(a, b, (((1,), (1,)), ((), ())), preferred_element_type=F32)


def _split2(x):
    hi = x.astype(BF16)
    lo = (x - hi.astype(F32)).astype(BF16)
    return hi, lo


def _split3(x):
    hi = x.astype(BF16)
    r = x - hi.astype(F32)
    mid = r.astype(BF16)
    lo = (r - mid.astype(F32)).astype(BF16)
    return hi, mid, lo


def _softplus(x):
    return jnp.maximum(x, 0.0) + jnp.log(1.0 + jnp.exp(-jnp.abs(x)))


def _sigmoid(x):
    return 1.0 / (1.0 + jnp.exp(-x))


def _ffn_kernel(x_ref, gpre_ref, wg_ref, wu_ref, wo_ref, gpost_ref, o_ref, xn_ref, acc_ref, *, nj):
    j = pl.program_id(1)

    @pl.when(j == 0)
    def _():
        xn_ref[...] = (_rms(x_ref[...]) * gpre_ref[...]).astype(BF16)
        acc_ref[...] = jnp.zeros_like(acc_ref)

    xn = xn_ref[...]
    g = _dot(xn, wg_ref[...])
    u = _dot(xn, wu_ref[...])
    a = (g * _sigmoid(g) * u).astype(BF16)
    acc_ref[...] += _dot(a, wo_ref[...])

    @pl.when(j == nj - 1)
    def _():
        o_ref[...] = x_ref[...] + 0.5 * (_rms(acc_ref[...]) * gpost_ref[...])


def _ffn(h, gpre, w_in, w_out, gpost, *, tm, tf):
    T, D = h.shape
    FF = w_out.shape[0]
    nj = FF // tf
    return pl.pallas_call(
        functools.partial(_ffn_kernel, nj=nj),
        grid=(T // tm, nj),
        in_specs=[
            pl.BlockSpec((tm, D), lambda i, j: (i, 0)),
            pl.BlockSpec((1, D), lambda i, j: (0, 0)),
            pl.BlockSpec((D, tf), lambda i, j: (0, j)),
            pl.BlockSpec((D, tf), lambda i, j: (0, j + nj)),
            pl.BlockSpec((tf, D), lambda i, j: (j, 0)),
            pl.BlockSpec((1, D), lambda i, j: (0, 0)),
        ],
        out_specs=pl.BlockSpec((tm, D), lambda i, j: (i, 0)),
        out_shape=jax.ShapeDtypeStruct((T, D), F32),
        scratch_shapes=[pltpu.VMEM((tm, D), BF16), pltpu.VMEM((tm, D), F32)],
        compiler_params=pltpu.CompilerParams(
            dimension_semantics=("parallel", "arbitrary"), vmem_limit_bytes=VMEM_LIMIT),
        name="ffn",
    )(h, gpre, w_in, w_in, w_out, gpost)


def _rope(x, c, sa, sb):
    half = ROT_DIM // 2
    return x * c + pltpu.roll(x, LANES - half, 1) * sa + pltpu.roll(x, half, 1) * sb


def _mix_in_kernel(h_ref, gpre_ref, w_ref, c_ref, sa_ref, sb_ref, kig_ref, smscale_ref,
                   qkva_ref, z_ref, qb_ref, kb_ref, vb_ref, qi_ref, g_ref, small_ref):
    xn = (_rms(h_ref[...]) * gpre_ref[...]).astype(BF16)
    c, sa, sb = c_ref[...], sa_ref[...], sb_ref[...]
    W = GROUP_W

    def proj(lo, hi):
        return _dot(xn, w_ref[:, lo:hi])

    qkva_ref[...] = proj(0, 3 * W)
    z_ref[...] = proj(3 * W, 4 * W)

    def roped(lo, scale):
        y = proj(lo, lo + W)
        parts = [_rope(y[:, t * LANES:(t + 1) * LANES], c, sa, sb) for t in range(W // LANES)]
        return (jnp.concatenate(parts, axis=1) * scale).astype(BF16)

    inv_sqrt_d = HEAD_DIM ** -0.5
    qb_ref[...] = roped(4 * W, inv_sqrt_d * LOG2_E)
    kb_ref[...] = roped(5 * W, 1.0)
    vb_ref[...] = proj(6 * W, 7 * W).astype(BF16)
    qi_ref[...] = roped(7 * W, inv_sqrt_d)
    g_ref[...] = proj(8 * W, 12 * W)

    sm = proj(12 * W, 12 * W + LANES)
    lane = lax.broadcasted_iota(I32, sm.shape, 1)
    is_ki = lane < HEAD_DIM
    ms = jnp.sum(jnp.where(is_ki, sm * sm, 0.0), axis=-1, keepdims=True) * (1.0 / HEAD_DIM)
    ki = _rope(sm * lax.rsqrt(ms + EPS) * kig_ref[...], c, sa, sb)
    small_ref[...] = jnp.where(is_ki, ki, sm * smscale_ref[...])


def _mix_in(h, gpre, w, rope_c, rope_sa, rope_sb, kig, smscale, *, tm, seq):
    T, D = h.shape
    W = GROUP_W
    ncol = w.shape[1]
    nseq = seq // tm
    row = lambda i: (i, 0)
    const = lambda i: (0, 0)
    pos = lambda i: (i % nseq, 0)
    outs = [(3 * W, F32), (W, F32), (W, BF16), (W, BF16), (W, BF16), (W, BF16), (4 * W, F32), (LANES, F32)]
    return pl.pallas_call(
        _mix_in_kernel,
        grid=(T // tm,),
        in_specs=[
            pl.BlockSpec((tm, D), row),
            pl.BlockSpec((1, D), const),
            pl.BlockSpec((D, ncol), const),
            pl.BlockSpec((tm, LANES), pos),
            pl.BlockSpec((tm, LANES), pos),
            pl.BlockSpec((tm, LANES), pos),
            pl.BlockSpec((1, LANES), const),
            pl.BlockSpec((1, LANES), const),
        ],
        out_specs=[pl.BlockSpec((tm, n), row) for n, _ in outs],
        out_shape=[jax.ShapeDtypeStruct((T, n), dt) for n, dt in outs],
        compiler_params=pltpu.CompilerParams(
            dimension_semantics=("parallel",), vmem_limit_bytes=VMEM_LIMIT),
        name="mix_in",
    )(h, gpre, w, rope_c, rope_sa, rope_sb, kig, smscale)


def _gdn_kernel(qkv_ref, z_ref, small_ref, convw_ref, alog_ref, dtb_ref, alogc_ref, dtbc_ref,
                dng_ref, bd_ref, o_ref,
                xs_ref, q3_ref, k3_ref, v3_ref, pw_ref, qkd_ref, sol_ref, u_ref, wd_ref,
                qd_ref, kdT_ref, oc_ref, state_ref, *, G, P, C):
    W = GROUP_W
    HD = HEAD_DIM
    H = N_HEADS
    U = G * H
    shift = int(math.log2(C))
    i = pl.program_id(1)

    @pl.when(i == 0)
    def _():
        for g in range(G):
            xs_ref[g, 0:8, :] = jnp.zeros((8, 3 * W), F32)
        state_ref[...] = jnp.zeros_like(state_ref)

    bd = bd_ref[...]

    def head_sum(x):
        return _dot(x.astype(BF16), bd)

    ri = lax.broadcasted_iota(I32, (P, P), 0)
    ci = lax.broadcasted_iota(I32, (P, P), 1)
    same = lax.shift_right_logical(ri, shift) == lax.shift_right_logical(ci, shift)
    incl = jnp.logical_and(same, ri >= ci)
    diag = ri == ci
    tri = jnp.where(incl, 1.0, 0.0).astype(BF16)
    triT = jnp.where(same, jnp.where(ci >= ri, 1.0, 0.0), 0.0).astype(BF16)
    blk = jnp.where(same, 1.0, 0.0).astype(BF16)

    gc, gcT, bfull, egc, ekd = [], [], [], [], []
    for g in range(G):
        xs_ref[g, 8:8 + P, :] = qkv_ref[g]
        w = convw_ref[...]
        y = w[CONV_TAPS - 1:CONV_TAPS, :] * xs_ref[g, 8:8 + P, :]
        for tap in range(CONV_TAPS - 1):
            off = 8 - (CONV_TAPS - 1) + tap
            y = y + w[tap:tap + 1, :] * xs_ref[g, off:off + P, :]
        xs_ref[g, 0:8, :] = xs_ref[g, P:P + 8, :]
        y = y * _sigmoid(y)
        q, k, v = y[:, :W], y[:, W:2 * W], y[:, 2 * W:]
        qn = q * lax.rsqrt(head_sum(q * q) + EPS) * (HD ** -0.5)
        kn = k * lax.rsqrt(head_sum(k * k) + EPS)
        for h in range(H):
            sl = slice(h * HD, (h + 1) * HD)
            q3_ref[g * H + h] = qn[:, sl]
            k3_ref[g * H + h] = kn[:, sl]
            v3_ref[g * H + h] = v[:, sl]

        sm = small_ref[g]
        gfull = -jnp.exp(alog_ref[...]) * _softplus(sm + dtb_ref[...])
        gT = -jnp.exp(alogc_ref[...]) * _softplus(sm.T[SM_A:SM_A + H, :] + dtbc_ref[...])
        g_parts = _split3(gfull)
        gc_g = sum(_dot(tri, part) for part in g_parts)
        gtot = sum(_dot(blk, part) for part in g_parts)
        gc.append(gc_g)
        gcT.append(sum(_dot(part, triT) for part in _split3(gT)))
        bfull.append(_sigmoid(sm))
        egc.append(jnp.exp(gc_g))
        ekd.append(jnp.exp(gtot - gc_g))

    for u in range(U):
        g, h = divmod(u, H)
        K, Q, V = k3_ref[u], q3_ref[u], v3_ref[u]
        Kb = K.astype(BF16)
        gcol = gc[g][:, SM_A + h:SM_A + h + 1]
        grow = gcT[g][h:h + 1, :]
        bcol = bfull[g][:, SM_B + h:SM_B + h + 1]
        ecol = egc[g][:, SM_A + h:SM_A + h + 1]
        decay = jnp.where(incl, jnp.exp(gcol - grow), 0.0)
        n_mat = jnp.where(diag, 0.0, -(bcol * _dot_nt(Kb, Kb) * decay)).astype(BF16)
        pw_ref[0, u] = n_mat
        qkd_ref[u] = (_dot_nt(Q.astype(BF16), Kb) * decay).astype(BF16)
        rhs = jnp.concatenate([V * bcol, K * (bcol * ecol)], axis=1)
        sol_ref[u] = rhs + _dot(n_mat, rhs.astype(BF16))
        qd_ref[u] = (Q * ecol).astype(BF16)
        kdT_ref[u] = (K * ekd[g][:, SM_A + h:SM_A + h + 1]).T.astype(BF16)

    for it in range(shift - 1):
        src, dst = it % 2, 1 - it % 2
        for u in range(U):
            pw = pw_ref[src, u]
            pw_ref[dst, u] = _dot(pw, pw).astype(BF16)
        for u in range(U):
            y = sol_ref[u]
            sol_ref[u] = y + _dot(pw_ref[dst, u], y.astype(BF16))
    for u in range(U):
        sol = sol_ref[u]
        u_ref[u] = sol[:, :HD]
        wd_ref[u] = sol[:, HD:].astype(BF16)

    for c in range(P // C):
        rows = slice(c * C, (c + 1) * C)
        states = [state_ref[u] for u in range(U)]
        sbf = [s.astype(BF16) for s in states]
        v_new = [(u_ref[u, rows, :] - _dot(wd_ref[u, rows, :], sbf[u])).astype(BF16) for u in range(U)]
        for u in range(U):
            g, h = divmod(u, H)
            oc_ref[g, rows, h * HD:(h + 1) * HD] = (
                _dot(qd_ref[u, rows, :], sbf[u]) + _dot(qkd_ref[u, rows, rows], v_new[u]))
        for u in range(U):
            g, h = divmod(u, H)
            g_tot = jnp.exp(gcT[g][h:h + 1, (c + 1) * C - 1:(c + 1) * C])
            state_ref[u] = states[u] * g_tot + _dot(kdT_ref[u, :, rows], v_new[u])

    for g in range(G):
        o = oc_ref[g]
        ms = head_sum(o * o) * (1.0 / HD)
        zz = z_ref[g]
        o_ref[g] = (o * lax.rsqrt(ms + EPS) * dng_ref[...] * (zz * _sigmoid(zz))).astype(BF16)


def _gdn(qkva, z, small, convw, alog_row, dtb_row, alog_col, dtb_col, dng, bd, *, batch, seq, G, P, C):
    W = GROUP_W
    H, HD = N_HEADS, HEAD_DIM
    U = G * H
    per_seq = lambda a: a.reshape(batch, seq, a.shape[-1])
    row = lambda b, i: (b, i, 0)
    const = lambda b, i: (0, 0)
    out = pl.pallas_call(
        functools.partial(_gdn_kernel, G=G, P=P, C=C),
        grid=(batch // G, seq // P),
        in_specs=[
            pl.BlockSpec((G, P, 3 * W), row),
            pl.BlockSpec((G, P, W), row),
            pl.BlockSpec((G, P, LANES), row),
            pl.BlockSpec((CONV_TAPS, 3 * W), const),
            pl.BlockSpec((1, LANES), const),
            pl.BlockSpec((1, LANES), const),
            pl.BlockSpec((H, 1), const),
            pl.BlockSpec((H, 1), const),
            pl.BlockSpec((1, W), const),
            pl.BlockSpec((W, W), const),
        ],
        out_specs=pl.BlockSpec((G, P, W), row),
        out_shape=jax.ShapeDtypeStruct((batch, seq, W), BF16),
        scratch_shapes=[
            pltpu.VMEM((G, P + 8, 3 * W), F32),
            pltpu.VMEM((U, P, HD), F32),
            pltpu.VMEM((U, P, HD), F32),
            pltpu.VMEM((U, P, HD), F32),
            pltpu.VMEM((2, U, P, P), BF16),
            pltpu.VMEM((U, P, P), BF16),
            pltpu.VMEM((U, P, 2 * HD), F32),
            pltpu.VMEM((U, P, HD), F32),
            pltpu.VMEM((U, P, HD), BF16),
            pltpu.VMEM((U, P, HD), BF16),
            pltpu.VMEM((U, HD, P), BF16),
            pltpu.VMEM((G, P, W), F32),
            pltpu.VMEM((U, HD, HD), F32),
        ],
        compiler_params=pltpu.CompilerParams(
            dimension_semantics=("parallel", "arbitrary"), vmem_limit_bytes=VMEM_LIMIT),
        name="gdn",
    )(per_seq(qkva), per_seq(z), per_seq(small), convw, alog_row, dtb_row, alog_col, dtb_col, dng, bd)
    return out.reshape(batch * seq, W)


def _key_to_f32(key):
    bits = key ^ (lax.shift_right_arithmetic(key, 31) & 0x7FFFFFFF)
    return pltpu.bitcast(bits, F32)


def _f32_to_key(x):
    bits = pltpu.bitcast(x, I32)
    return bits ^ (lax.shift_right_arithmetic(bits, 31) & 0x7FFFFFFF)


def _dsa_kernel(qb_ref, qi_ref, wsm_ref, kb_ref, vb_ref, ksm_ref, o_ref,
                sc_ref, vT_ref, kib_ref, s_ref, acc_ref, knorm_ref, *, Tq, Tk, seq, topk):
    HD = HEAD_DIM
    H = N_HEADS
    W = GROUP_W
    ACC = 2 * SUBLANES
    i = pl.program_id(1)
    nkt = ((i + 1) * Tq + Tk - 1) // Tk
    qpos = i * Tq + lax.broadcasted_iota(I32, (1, Tq), 1)
    sub_pos = lax.broadcasted_iota(I32, (Tk, Tq), 0)
    head_rows = lax.broadcasted_iota(I32, (H, W), 0)
    head_of_lane = lax.shift_right_logical(lax.broadcasted_iota(I32, (H, W), 1), int(math.log2(HD)))
    head_sel = jnp.where(head_rows == head_of_lane, 1.0, 0.0).astype(BF16)

    @pl.when(i == 0)
    def _():
        k_norm2 = jnp.zeros((H, 1), F32)
        for kt in range(seq // Tk):
            rows = slice(kt * Tk, (kt + 1) * Tk)
            vT_ref[kt] = vb_ref[rows, :].astype(F32).T.astype(BF16)
            k_tile = kb_ref[rows, :].astype(F32)
            per_head = _dot_nt(head_sel, (k_tile * k_tile).astype(BF16))
            k_norm2 = jnp.maximum(k_norm2, jnp.max(per_head, axis=1, keepdims=True))
        knorm_ref[...] = jnp.broadcast_to(jnp.sqrt(k_norm2), knorm_ref.shape)
        kib_ref[...] = ksm_ref[...].astype(BF16)

    wT = wsm_ref[...].T
    qi_heads = [qi_ref[:, h * HD:(h + 1) * HD] for h in range(H)]
    w_rows = [wT[SM_WI + h:SM_WI + h + 1, :] for h in range(H)]

    def fold(m):
        return jnp.sum(m.reshape(m.shape[0] // ACC, ACC, Tq), axis=0)

    def score_body(kt, carry):
        rmax, rmin, ge0, gt0 = carry
        r0 = pl.multiple_of(kt * Tk, Tk)
        ki = kib_ref[pl.ds(r0, Tk), :][:, SM_KI:SM_KI + HD]
        dots = [_dot_nt(ki, qi_heads[h]) for h in range(H)]
        score = jnp.zeros((Tk, Tq), F32)
        for h in range(H):
            score = score + w_rows[h] * jnp.maximum(dots[h], 0.0)
        adm = r0 + sub_pos <= qpos
        sc = jnp.where(adm, score, -jnp.inf)
        sc_ref[pl.ds(r0, Tk), :] = sc
        rmax = jnp.maximum(rmax, jnp.max(sc, axis=0, keepdims=True))
        rmin = jnp.minimum(rmin, jnp.min(jnp.where(adm, score, jnp.inf), axis=0, keepdims=True))
        ge0 = ge0 + fold(jnp.where(sc >= 0.0, 1.0, 0.0))
        gt0 = gt0 + fold(jnp.where(sc > 0.0, 1.0, 0.0))
        return rmax, rmin, ge0, gt0

    row = lambda v: jnp.full((1, Tq), v, F32)
    rmax, rmin, ge0, gt0 = lax.fori_loop(
        0, nkt, score_body, (row(-jnp.inf), row(jnp.inf), jnp.zeros((ACC, Tq), F32), jnp.zeros((ACC, Tq), F32)))
    cnt_ge0 = jnp.sum(ge0, axis=0, keepdims=True)
    cnt_gt0 = jnp.sum(gt0, axis=0, keepdims=True)

    def count(pred):
        def body(j, acc):
            r0 = pl.multiple_of(j * COUNT_ROWS, COUNT_ROWS)
            return acc + fold(pred(r0, sc_ref[pl.ds(r0, COUNT_ROWS), :]))
        acc = lax.fori_loop(0, nkt * (Tk // COUNT_ROWS), body, jnp.zeros((ACC, Tq), F32))
        return jnp.sum(acc, axis=0, keepdims=True)

    def count_ge(v):
        return count(lambda r0, s: jnp.where(s >= v, 1.0, 0.0))

    def count_gt(v):
        return count(lambda r0, s: jnp.where(s > v, 1.0, 0.0))

    kf = float(topk)
    log_k = math.log(kf)
    short = qpos < topk
    zero_tie = jnp.logical_and(cnt_gt0 < kf, cnt_ge0 >= kf)
    positive = cnt_gt0 >= kf
    tiny = row(FLT_MIN_NORMAL)
    lo_k = jnp.where(positive, _f32_to_key(tiny), _f32_to_key(rmin))
    hi_k = jnp.where(positive, _f32_to_key(rmax) + 1, _f32_to_key(-tiny))
    c_lo = jnp.where(positive, cnt_gt0, (qpos + 1).astype(F32))
    c_hi = jnp.where(positive, 0.0, cnt_ge0)
    done0 = jnp.where(jnp.logical_or(jnp.logical_or(short, zero_tie), c_lo == kf), 1.0, 0.0)

    def probe(st):
        it, lo_k, hi_k, c_lo, c_hi, w_lo, w_hi, last, done = st
        lo = _key_to_f32(lo_k)
        hi = _key_to_f32(hi_k - 1)
        a = (jnp.log(c_lo + 0.5) - log_k) * w_lo
        b = (log_k - jnp.log(c_hi + 0.5)) * w_hi
        frac = jnp.clip(a / jnp.maximum(a + b, 1e-30), 0.02, 0.98)
        cand_k = _f32_to_key(lo + frac * (hi - lo))
        mid_k = lo_k + lax.shift_right_logical(hi_k - lo_k, 1)
        cand_k = jnp.where(jnp.logical_and(it >= SEARCH_FAST_PASSES, it % 4 == 3), mid_k, cand_k)
        cand_k = jnp.minimum(jnp.maximum(cand_k, lo_k + 1), hi_k - 1)
        c = count_ge(_key_to_f32(cand_k))
        live = done == 0.0
        up = jnp.logical_and(live, c >= kf)
        dn = jnp.logical_and(live, c < kf)
        w_hi = jnp.where(jnp.logical_and(up, last > 0.0), w_hi * 0.5, jnp.where(dn, 1.0, w_hi))
        w_lo = jnp.where(jnp.logical_and(dn, last < 0.0), w_lo * 0.5, jnp.where(up, 1.0, w_lo))
        last = jnp.where(up, 1.0, jnp.where(dn, -1.0, last))
        lo_k = jnp.where(up, cand_k, lo_k)
        c_lo = jnp.where(up, c, c_lo)
        hi_k = jnp.where(dn, cand_k, hi_k)
        c_hi = jnp.where(dn, c, c_hi)
        done = jnp.where(jnp.logical_or(c_lo == kf, hi_k - lo_k <= 1), 1.0, done)
        return it + 1, lo_k, hi_k, c_lo, c_hi, w_lo, w_hi, last, done

    def search_cond(st):
        far = jnp.where(jnp.logical_and(st[-1] == 0.0, kf - st[4] > 1.0), 1.0, 0.0)
        return jnp.logical_and(st[0] < SEARCH_MAX_PASSES, jnp.max(far) > 0.0)

    st = (jnp.int32(0), lo_k, hi_k, c_lo, c_hi, row(1.0), row(1.0), row(0.0), done0)
    st = lax.while_loop(search_cond, lambda st: probe(probe(st)), st)
    _, lo_k, hi_k, c_lo, c_hi = st[:5]
    open_rows = st[-1] == 0.0

    lo_f = jnp.where(short, -jnp.inf, jnp.where(zero_tie, 0.0, _key_to_f32(lo_k)))
    hi_f = jnp.where(zero_tie, tiny, _key_to_f32(hi_k))
    c_lo = jnp.where(zero_tie, cnt_ge0, c_lo)
    c_hi = jnp.where(zero_tie, cnt_gt0, c_hi)

    def below_body(kt, best):
        r0 = pl.multiple_of(kt * Tk, Tk)
        s = sc_ref[pl.ds(r0, Tk), :]
        cand = jnp.where(s < hi_f, s, -jnp.inf)
        return jnp.maximum(best, jnp.max(cand.reshape(Tk // ACC, ACC, Tq), axis=0))

    below = jnp.max(lax.fori_loop(0, nkt, below_body, jnp.full((ACC, Tq), -jnp.inf, F32)),
                    axis=0, keepdims=True)
    lo_f = jnp.where(open_rows, below, lo_f)
    c_lo = jnp.where(open_rows, c_hi + 1.0, c_lo)
    cnt_gt = count_gt(lo_f)

    def unresolved(lo_f, hi_f, c_lo, c_hi, cnt_gt):
        inside = c_lo - c_hi
        u = jnp.logical_and(kf - c_hi < inside, c_lo - cnt_gt < inside)
        return jnp.where(short, 0.0, jnp.where(u, 1.0, 0.0))

    def refine_cond(st):
        return jnp.logical_and(st[0] < SEARCH_MAX_PASSES, jnp.max(unresolved(*st[1:])) > 0.0)

    def refine(st):
        it, lo_f, hi_f, c_lo, c_hi, cnt_gt = st
        live = unresolved(lo_f, hi_f, c_lo, c_hi, cnt_gt) > 0.0
        mid = 0.5 * lo_f + 0.5 * hi_f
        c = count_ge(mid)
        up = jnp.logical_and(live, c >= kf)
        dn = jnp.logical_and(live, c < kf)
        lo_f = jnp.where(up, mid, lo_f)
        c_lo = jnp.where(up, c, c_lo)
        hi_f = jnp.where(dn, mid, hi_f)
        c_hi = jnp.where(dn, c, c_hi)
        return it + 1, lo_f, hi_f, c_lo, c_hi, count_gt(lo_f)

    _, thr, _, c_thr, _, cnt_gt = lax.while_loop(
        refine_cond, refine, (jnp.int32(0), lo_f, hi_f, c_lo, c_hi, cnt_gt))

    quota = jnp.where(short, 0.0, kf - cnt_gt)
    before = jnp.where(lax.broadcasted_iota(I32, (Tk, Tk), 0) > lax.broadcasted_iota(I32, (Tk, Tk), 1),
                       1.0, 0.0).astype(BF16)

    def mask_body(kt, seen):
        r0 = pl.multiple_of(kt * Tk, Tk)
        sc = sc_ref[pl.ds(r0, Tk), :]
        eq = jnp.where(sc == thr, 1.0, 0.0)
        rank = _dot(before, eq.astype(BF16)) + seen
        tie = jnp.where(rank < quota, 0.0, -jnp.inf)
        sc_ref[pl.ds(r0, Tk), :] = jnp.where(sc > thr, 0.0, jnp.where(sc == thr, tie, -jnp.inf))
        return seen + jnp.sum(eq, axis=0, keepdims=True)

    lax.fori_loop(0, nkt, mask_body, jnp.zeros((1, Tq), F32))

    q_all = qb_ref[...].astype(F32)
    q_norm = jnp.sqrt(_dot_nt(head_sel, (q_all * q_all).astype(BF16)))
    shift = [q_norm[h:h + 1, :] * (ATT_BOUND_MARGIN * knorm_ref[h:h + 1, 0:1]) for h in range(H)]

    def att_fast(kt, l_all):
        r0 = pl.multiple_of(kt * Tk, Tk)
        bias = sc_ref[pl.ds(r0, Tk), :]
        for h in range(H):
            sl = slice(h * HD, (h + 1) * HD)
            s_ref[h] = _dot_nt(kb_ref[pl.ds(r0, Tk), sl], qb_ref[:, sl]) + bias - shift[h]
        l_rows = []
        for h in range(H):
            p = jnp.exp2(s_ref[h])
            l_rows.append(l_all[h:h + 1, :] + jnp.sum(p, axis=0, keepdims=True))
            acc_ref[h] += _dot(vT_ref[kt, h * HD:(h + 1) * HD, :], p.astype(BF16))
        return jnp.concatenate(l_rows, axis=0)

    acc_ref[...] = jnp.zeros_like(acc_ref)
    l_fast = lax.fori_loop(0, nkt, att_fast, jnp.zeros((H, Tq), F32))

    def att_slow(kt, carry):
        m_all, l_all = carry
        r0 = pl.multiple_of(kt * Tk, Tk)
        bias = sc_ref[pl.ds(r0, Tk), :]
        tile_max = []
        for h in range(H):
            sl = slice(h * HD, (h + 1) * HD)
            s = _dot_nt(kb_ref[pl.ds(r0, Tk), sl], qb_ref[:, sl]) + bias
            s_ref[h] = s
            tile_max.append(jnp.max(s, axis=0, keepdims=True))
        m_rows, l_rows = [], []
        for h in range(H):
            s = s_ref[h]
            m_old = m_all[h:h + 1, :]
            m_new = jnp.maximum(m_old, tile_max[h])
            alpha = jnp.exp2(m_old - m_new)
            p = jnp.exp2(s - m_new)
            l_rows.append(alpha * l_all[h:h + 1, :] + jnp.sum(p, axis=0, keepdims=True))
            acc_ref[h] = alpha * acc_ref[h] + _dot(vT_ref[kt, h * HD:(h + 1) * HD, :], p.astype(BF16))
            m_rows.append(m_new)
        return jnp.concatenate(m_rows, axis=0), jnp.concatenate(l_rows, axis=0)

    def redo():
        acc_ref[...] = jnp.zeros_like(acc_ref)
        init = (jnp.full((H, Tq), -1e30, F32), jnp.zeros((H, Tq), F32))
        return lax.fori_loop(0, nkt, att_slow, init)[1]

    l_fin = lax.cond(jnp.min(l_fast) > ATT_MIN_DENOM, lambda: l_fast, redo)
    out_t = jnp.concatenate([acc_ref[h] / l_fin[h:h + 1, :] for h in range(H)], axis=0)
    o_ref[...] = out_t.T.astype(BF16)


def _dsa(qb, kb, vb, qi, small, *, batch, seq, Tq, Tk):
    T = qb.shape[0]
    W = GROUP_W
    nq = seq // Tq
    qrow = lambda b, i: (b * nq + i, 0)
    full = lambda b, i: (b, 0)
    topk = min(TOPK_MAX, seq // 4)
    return pl.pallas_call(
        functools.partial(_dsa_kernel, Tq=Tq, Tk=Tk, seq=seq, topk=topk),
        grid=(batch, nq),
        in_specs=[
            pl.BlockSpec((Tq, W), qrow),
            pl.BlockSpec((Tq, W), qrow),
            pl.BlockSpec((Tq, LANES), qrow),
            pl.BlockSpec((seq, W), full),
            pl.BlockSpec((seq, W), full),
            pl.BlockSpec((seq, LANES), full),
        ],
        out_specs=pl.BlockSpec((Tq, W), qrow),
        out_shape=jax.ShapeDtypeStruct((T, W), BF16),
        scratch_shapes=[
            pltpu.VMEM((seq, Tq), F32),
            pltpu.VMEM((seq // Tk, W, Tk), BF16),
            pltpu.VMEM((seq, LANES), BF16),
            pltpu.VMEM((N_HEADS, Tk, Tq), F32),
            pltpu.VMEM((N_HEADS, HEAD_DIM, Tq), F32),
            pltpu.VMEM((N_HEADS, LANES), F32),
        ],
        compiler_params=pltpu.CompilerParams(
            dimension_semantics=("parallel", "arbitrary"), vmem_limit_bytes=VMEM_LIMIT),
        name="dsa",
    )(qb, qi, small, kb, vb, small)


def _merge_kernel(h_ref, oa_ref, ob_ref, g_ref, wa_ref, wb_ref, wo_ref, gpost_ref, o_ref):
    D = h_ref.shape[1]
    ya = _dot(oa_ref[...], wa_ref[...])
    yb = _dot(ob_ref[...], wb_ref[...])
    g = g_ref[...]
    merged = _sigmoid(g[:, :D]) * ya + _sigmoid(g[:, D:]) * yb
    mix = _dot(merged.astype(BF16), wo_ref[...])
    o_ref[...] = h_ref[...] + _rms(mix) * gpost_ref[...]


def _merge(h, oa, ob, g, wa, wb, wo, gpost, *, tm):
    T, D = h.shape
    W = GROUP_W
    row = lambda i: (i, 0)
    const = lambda i: (0, 0)
    return pl.pallas_call(
        _merge_kernel,
        grid=(T // tm,),
        in_specs=[
            pl.BlockSpec((tm, D), row),
            pl.BlockSpec((tm, W), row),
            pl.BlockSpec((tm, W), row),
            pl.BlockSpec((tm, 2 * D), row),
            pl.BlockSpec((W, D), const),
            pl.BlockSpec((W, D), const),
            pl.BlockSpec((D, D), const),
            pl.BlockSpec((1, D), const),
        ],
        out_specs=pl.BlockSpec((tm, D), row),
        out_shape=jax.ShapeDtypeStruct((T, D), F32),
        compiler_params=pltpu.CompilerParams(
            dimension_semantics=("parallel",), vmem_limit_bytes=VMEM_LIMIT),
        name="merge",
    )(h, oa, ob, g, wa, wb, wo, gpost)


def _ple_kernel(h_ref, p_ref, gpre_ref, wg_ref, wp_ref, gpost_ref, o_ref):
    h = h_ref[...]
    gate = _sigmoid(_dot((_rms(h) * gpre_ref[...]).astype(BF16), wg_ref[...]))
    e = _dot(p_ref[...].astype(BF16), wp_ref[...])
    o_ref[...] = h + _rms(gate * e) * gpost_ref[...]


def _ple(h, p, gpre, wg, wp, gpost, *, tm):
    T, D = h.shape
    P = p.shape[1]
    row = lambda i: (i, 0)
    const = lambda i: (0, 0)
    return pl.pallas_call(
        _ple_kernel,
        grid=(T // tm,),
        in_specs=[
            pl.BlockSpec((tm, D), row),
            pl.BlockSpec((tm, P), row),
            pl.BlockSpec((1, D), const),
            pl.BlockSpec((D, D), const),
            pl.BlockSpec((P, D), const),
            pl.BlockSpec((1, D), const),
        ],
        out_specs=pl.BlockSpec((tm, D), row),
        out_shape=jax.ShapeDtypeStruct((T, D), F32),
        compiler_params=pltpu.CompilerParams(
            dimension_semantics=("parallel",), vmem_limit_bytes=VMEM_LIMIT),
        name="ple",
    )(h, p, gpre, wg, wp, gpost)


def _rope_tables(seq):
    half = ROT_DIM // 2
    pos = jnp.arange(seq, dtype=F32)
    inv = ROPE_THETA ** (-jnp.arange(0, ROT_DIM, 2, dtype=F32) / ROT_DIM)
    ang = pos[:, None] * inv[None, :]
    cos, sin = jnp.cos(ang), jnp.sin(ang)
    ones = jnp.ones((seq, HEAD_DIM - ROT_DIM), F32)
    zeros_h = jnp.zeros((seq, half), F32)
    zeros_r = jnp.zeros((seq, HEAD_DIM - ROT_DIM), F32)
    c = jnp.concatenate([cos, cos, ones], axis=1)
    sa = jnp.concatenate([-sin, zeros_h, zeros_r], axis=1)
    sb = jnp.concatenate([zeros_h, sin, zeros_r], axis=1)
    rep = LANES // HEAD_DIM
    return jnp.tile(c, (1, rep)), jnp.tile(sa, (1, rep)), jnp.tile(sb, (1, rep))


def _pad_lanes(vec, offset):
    out = jnp.zeros((1, LANES), F32)
    return lax.dynamic_update_slice(out, vec.reshape(1, -1).astype(F32), (0, offset))


def kernel(x, p, ffn1_norm_pre, ffn1_norm_post, ffn1_w_in, ffn1_w_out, mix_norm_pre, mix_norm_post, mix_w_in, conv_w, a_log, dt_bias, dn_norm_g, idx_k_norm_g, w_br_a, w_br_b, mix_w_out, ffn2_norm_pre, ffn2_norm_post, ffn2_w_in, ffn2_w_out, ple_norm_pre, ple_norm_post, ple_w_gate, ple_w_proj):
    B, S, D = x.shape
    T = B * S
    W = GROUP_W
    depth = ffn1_w_in.shape[0]
    ff = ffn1_w_out.shape[1]

    tm_ffn = min(512, T)
    tf = ff // 2 if (ff // 2) % LANES == 0 else ff
    tm_mix = min(256, S)
    gdn_rows = min(128, S)
    gdn_seqs = 2 if B % 2 == 0 else 1
    gdn_chunk = min(64, gdn_rows)
    Tq = min(512, S)
    Tk = min(512, S)
    tm_post = min(512, T)

    rope_c, rope_sa, rope_sb = _rope_tables(S)
    ri = jnp.arange(W)
    bd = (ri[:, None] // HEAD_DIM == ri[None, :] // HEAD_DIM).astype(BF16)
    smscale = _pad_lanes(jnp.full((N_HEADS,), N_HEADS ** -0.5, F32), SM_WI)
    smscale = jnp.where(smscale == 0.0, 1.0, smscale)
    vec = lambda a: a.reshape(1, -1).astype(F32)

    h = x.reshape(T, D)
    for l in range(depth):
        h = _ffn(h, vec(ffn1_norm_pre[l]), ffn1_w_in[l].astype(BF16), ffn1_w_out[l].astype(BF16),
                 vec(ffn1_norm_post[l]), tm=tm_ffn, tf=tf)

        wi = mix_w_in[l]
        o_a, o_b = 4 * W, 4 * W + 2 * N_HEADS
        o_ki = o_b + 4 * W
        o_wi = o_ki + HEAD_DIM
        o_g = o_wi + N_HEADS
        w_cat = jnp.concatenate([
            wi[:, :4 * W], wi[:, o_b:o_ki], wi[:, o_g:o_g + 2 * D],
            wi[:, o_ki:o_g], wi[:, o_a:o_b],
            jnp.zeros((D, LANES - HEAD_DIM - 3 * N_HEADS), wi.dtype)], axis=1).astype(BF16)
        qkva, z, qb, kb, vb, qi, g, small = _mix_in(
            h, vec(mix_norm_pre[l]), w_cat, rope_c, rope_sa, rope_sb,
            _pad_lanes(idx_k_norm_g[l], SM_KI), smscale, tm=tm_mix, seq=S)

        o_gdn = _gdn(qkva, z, small, conv_w[l].astype(F32),
                     _pad_lanes(a_log[l], SM_A), _pad_lanes(dt_bias[l], SM_A),
                     a_log[l].reshape(N_HEADS, 1).astype(F32), dt_bias[l].reshape(N_HEADS, 1).astype(F32),
                     jnp.tile(vec(dn_norm_g[l]), (1, N_HEADS)), bd,
                     batch=B, seq=S, G=gdn_seqs, P=gdn_rows, C=gdn_chunk)
        o_dsa = _dsa(qb, kb, vb, qi, small, batch=B, seq=S, Tq=Tq, Tk=Tk)

        h = _merge(h, o_gdn, o_dsa, g, w_br_a[l].astype(BF16), w_br_b[l].astype(BF16),
                   mix_w_out[l].astype(BF16), vec(mix_norm_post[l]), tm=tm_post)
        h = _ffn(h, vec(ffn2_norm_pre[l]), ffn2_w_in[l].astype(BF16), ffn2_w_out[l].astype(BF16),
                 vec(ffn2_norm_post[l]), tm=tm_ffn, tf=tf)
        h = _ple(h, p[l].reshape(T, -1), vec(ple_norm_pre[l]), ple_w_gate[l].astype(BF16),
                 ple_w_proj[l].astype(BF16), vec(ple_norm_post[l]), tm=tm_post)
    return h.reshape(B, S, D)
```

```python
import functools
import math

import jax
import jax.numpy as jnp
from jax import lax
from jax.experimental import pallas as pl
from jax.experimental.pallas import tpu as pltpu

F32 = jnp.float32
BF16 = jnp.bfloat16
I32 = jnp.int32

EPS = 1e-6
N_HEADS = 8
HEAD_DIM = 64
GROUP_W = N_HEADS * HEAD_DIM
CONV_TAPS = 4
ROT_DIM = HEAD_DIM // 4
ROPE_THETA = 500000.0
TOPK_MAX = 256
LANES = 128
SUBLANES = 8
MXU_DEPTH = 256
FLT_MIN_NORMAL = 1.1754943508222875e-38
LOG2_E = 1.4426950408889634
SEARCH_FAST_PASSES = 24
SEARCH_MAX_PASSES = 256
COUNT_ROWS = 128
ATT_BOUND_MARGIN = 1.02
ATT_MIN_DENOM = 1e-30
VMEM_LIMIT = 56 * 1024 * 1024

SM_KI = 0
SM_WI = 64
SM_A = 72
SM_B = 80


def _rms(x):
    return x * lax.rsqrt(jnp.mean(x * x, axis=-1, keepdims=True) + EPS)


def _dot(a, b):
    return jnp.dot(a, b, preferred_element_type=F32)


def _dot_nt(a, b):
    return lax.dot_general(a, b, (((1,), (1,)), ((), ())), preferred_element_type=F32)


def _split2(x):
    hi = x.astype(BF16)
    lo = (x - hi.astype(F32)).astype(BF16)
    return hi, lo


def _split3(x):
    hi = x.astype(BF16)
    r = x - hi.astype(F32)
    mid = r.astype(BF16)
    lo = (r - mid.astype(F32)).astype(BF16)
    return hi, mid, lo


def _softplus(x):
    return jnp.maximum(x, 0.0) + jnp.log(1.0 + jnp.exp(-jnp.abs(x)))


def _sigmoid(x):
    return 1.0 / (1.0 + jnp.exp(-x))


def _ffn_kernel(x_ref, gpre_ref, w_in_ref, w_out_ref, gpost_ref, o_ref, *, splits):
    ff = w_out_ref.shape[0]
    x = x_ref[...]
    xn = (_rms(x) * gpre_ref[...]).astype(BF16)
    acc = None
    for lo, hi in splits:
        g = _dot(xn, w_in_ref[:, lo:hi])
        u = _dot(xn, w_in_ref[:, ff + lo:ff + hi])
        part = _dot((g * _sigmoid(g) * u).astype(BF16), w_out_ref[lo:hi, :])
        acc = part if acc is None else acc + part
    o_ref[...] = x + 0.5 * (_rms(acc) * gpost_ref[...])


def _ffn(h, gpre, w_in, w_out, gpost, *, tm, n_split):
    T, D = h.shape
    FF = w_out.shape[0]
    step = -(-FF // (n_split * MXU_DEPTH)) * MXU_DEPTH
    splits = tuple((lo, min(lo + step, FF)) for lo in range(0, FF, step))
    resident = dict(pipeline_mode=pl.Buffered(1))
    return pl.pallas_call(
        functools.partial(_ffn_kernel, splits=splits),
        grid=(T // tm,),
        in_specs=[
            pl.BlockSpec((tm, D), lambda i: (i, 0)),
            pl.BlockSpec((1, D), lambda i: (0, 0)),
            pl.BlockSpec((D, 2 * FF), lambda i: (0, 0), **resident),
            pl.BlockSpec((FF, D), lambda i: (0, 0), **resident),
            pl.BlockSpec((1, D), lambda i: (0, 0)),
        ],
        out_specs=pl.BlockSpec((tm, D), lambda i: (i, 0)),
        out_shape=jax.ShapeDtypeStruct((T, D), F32),
        compiler_params=pltpu.CompilerParams(
            dimension_semantics=("parallel",), vmem_limit_bytes=VMEM_LIMIT),
        name="ffn",
    )(h, gpre, w_in, w_out, gpost)


def _rope(x, c, sa, sb):
    half = ROT_DIM // 2
    return x * c + pltpu.roll(x, LANES - half, 1) * sa + pltpu.roll(x, half, 1) * sb


def _mix_in_kernel(h_ref, gpre_ref, w_ref, c_ref, sa_ref, sb_ref, kig_ref, smscale_ref,
                   qkva_ref, z_ref, qb_ref, kb_ref, vb_ref, qi_ref, g_ref, small_ref):
    xn = (_rms(h_ref[...]) * gpre_ref[...]).astype(BF16)
    c, sa, sb = c_ref[...], sa_ref[...], sb_ref[...]
    W = GROUP_W

    def proj(lo, hi):
        return _dot(xn, w_ref[:, lo:hi])

    qkva_ref[...] = proj(0, 3 * W)
    z_ref[...] = proj(3 * W, 4 * W)

    def roped(lo, scale):
        y = proj(lo, lo + W)
        parts = [_rope(y[:, t * LANES:(t + 1) * LANES], c, sa, sb) for t in range(W // LANES)]
        return (jnp.concatenate(parts, axis=1) * scale).astype(BF16)

    inv_sqrt_d = HEAD_DIM ** -0.5
    qb_ref[...] = roped(4 * W, inv_sqrt_d * LOG2_E)
    kb_ref[...] = roped(5 * W, 1.0)
    vb_ref[...] = proj(6 * W, 7 * W).astype(BF16)
    qi_ref[...] = roped(7 * W, inv_sqrt_d)
    g_ref[...] = proj(8 * W, 12 * W)

    sm = proj(12 * W, 12 * W + LANES)
    lane = lax.broadcasted_iota(I32, sm.shape, 1)
    is_ki = lane < HEAD_DIM
    ms = jnp.sum(jnp.where(is_ki, sm * sm, 0.0), axis=-1, keepdims=True) * (1.0 / HEAD_DIM)
    ki = _rope(sm * lax.rsqrt(ms + EPS) * kig_ref[...], c, sa, sb)
    small_ref[...] = jnp.where(is_ki, ki, sm * smscale_ref[...])


def _mix_in(h, gpre, w, rope_c, rope_sa, rope_sb, kig, smscale, *, tm, seq):
    T, D = h.shape
    W = GROUP_W
    ncol = w.shape[1]
    nseq = seq // tm
    row = lambda i: (i, 0)
    const = lambda i: (0, 0)
    pos = lambda i: (i % nseq, 0)
    outs = [(3 * W, F32), (W, F32), (W, BF16), (W, BF16), (W, BF16), (W, BF16), (4 * W, F32), (LANES, F32)]
    return pl.pallas_call(
        _mix_in_kernel,
        grid=(T // tm,),
        in_specs=[
            pl.BlockSpec((tm, D), row),
            pl.BlockSpec((1, D), const),
            pl.BlockSpec((D, ncol), const, pipeline_mode=pl.Buffered(1)),
            pl.BlockSpec((tm, LANES), pos),
            pl.BlockSpec((tm, LANES), pos),
            pl.BlockSpec((tm, LANES), pos),
            pl.BlockSpec((1, LANES), const),
            pl.BlockSpec((1, LANES), const),
        ],
        out_specs=[pl.BlockSpec((tm, n), row) for n, _ in outs],
        out_shape=[jax.ShapeDtypeStruct((T, n), dt) for n, dt in outs],
        compiler_params=pltpu.CompilerParams(
            dimension_semantics=("parallel",), vmem_limit_bytes=VMEM_LIMIT),
        name="mix_in",
    )(h, gpre, w, rope_c, rope_sa, rope_sb, kig, smscale)


def _gdn_kernel(qkv_ref, z_ref, small_ref, convw_ref, alog_ref, dtb_ref, alogc_ref, dtbc_ref,
                dng_ref, bd_ref, o_ref,
                xs_ref, q3_ref, k3_ref, v3_ref, pw_ref, qkd_ref, sol_ref, u_ref, wd_ref,
                qd_ref, kdT_ref, oc_ref, state_ref, *, G, P, C):
    W = GROUP_W
    HD = HEAD_DIM
    H = N_HEADS
    U = G * H
    shift = int(math.log2(C))
    i = pl.program_id(1)

    @pl.when(i == 0)
    def _():
        for g in range(G):
            xs_ref[g, 0:8, :] = jnp.zeros((8, 3 * W), F32)
        state_ref[...] = jnp.zeros_like(state_ref)

    bd = bd_ref[...]

    def head_sum(x):
        return _dot(x.astype(BF16), bd)

    ri = lax.broadcasted_iota(I32, (P, P), 0)
    ci = lax.broadcasted_iota(I32, (P, P), 1)
    same = lax.shift_right_logical(ri, shift) == lax.shift_right_logical(ci, shift)
    incl = jnp.logical_and(same, ri >= ci)
    diag = ri == ci
    tri = jnp.where(incl, 1.0, 0.0).astype(BF16)
    triT = jnp.where(same, jnp.where(ci >= ri, 1.0, 0.0), 0.0).astype(BF16)
    blk = jnp.where(same, 1.0, 0.0).astype(BF16)

    gc, gcT, bfull, egc, ekd = [], [], [], [], []
    for g in range(G):
        xs_ref[g, 8:8 + P, :] = qkv_ref[g]
        w = convw_ref[...]
        y = w[CONV_TAPS - 1:CONV_TAPS, :] * xs_ref[g, 8:8 + P, :]
        for tap in range(CONV_TAPS - 1):
            off = 8 - (CONV_TAPS - 1) + tap
            y = y + w[tap:tap + 1, :] * xs_ref[g, off:off + P, :]
        xs_ref[g, 0:8, :] = xs_ref[g, P:P + 8, :]
        y = y * _sigmoid(y)
        q, k, v = y[:, :W], y[:, W:2 * W], y[:, 2 * W:]
        qn = q * lax.rsqrt(head_sum(q * q) + EPS) * (HD ** -0.5)
        kn = k * lax.rsqrt(head_sum(k * k) + EPS)
        for h in range(H):
            sl = slice(h * HD, (h + 1) * HD)
            q3_ref[g * H + h] = qn[:, sl]
            k3_ref[g * H + h] = kn[:, sl]
            v3_ref[g * H + h] = v[:, sl]

        sm = small_ref[g]
        gfull = -jnp.exp(alog_ref[...]) * _softplus(sm + dtb_ref[...])
        gT = -jnp.exp(alogc_ref[...]) * _softplus(sm.T[SM_A:SM_A + H, :] + dtbc_ref[...])
        g_parts = _split3(gfull)
        gc_g = sum(_dot(tri, part) for part in g_parts)
        gtot = sum(_dot(blk, part) for part in g_parts)
        gc.append(gc_g)
        gcT.append(sum(_dot(part, triT) for part in _split3(gT)))
        bfull.append(_sigmoid(sm))
        egc.append(jnp.exp(gc_g))
        ekd.append(jnp.exp(gtot - gc_g))

    for u in range(U):
        g, h = divmod(u, H)
        K, Q, V = k3_ref[u], q3_ref[u], v3_ref[u]
        Kb = K.astype(BF16)
        gcol = gc[g][:, SM_A + h:SM_A + h + 1]
        grow = gcT[g][h:h + 1, :]
        bcol = bfull[g][:, SM_B + h:SM_B + h + 1]
        ecol = egc[g][:, SM_A + h:SM_A + h + 1]
        decay = jnp.where(incl, jnp.exp(gcol - grow), 0.0)
        n_mat = jnp.where(diag, 0.0, -(bcol * _dot_nt(Kb, Kb) * decay)).astype(BF16)
        pw_ref[0, u] = n_mat
        qkd_ref[u] = (_dot_nt(Q.astype(BF16), Kb) * decay).astype(BF16)
        rhs = jnp.concatenate([V * bcol, K * (bcol * ecol)], axis=1)
        sol_ref[u] = rhs + _dot(n_mat, rhs.astype(BF16))
        qd_ref[u] = (Q * ecol).astype(BF16)
        kdT_ref[u] = (K * ekd[g][:, SM_A + h:SM_A + h + 1]).T.astype(BF16)

    for it in range(shift - 1):
        src, dst = it % 2, 1 - it % 2
        for u in range(U):
            pw = pw_ref[src, u]
            pw_ref[dst, u] = _dot(pw, pw).astype(BF16)
        for u in range(U):
            y = sol_ref[u]
            sol_ref[u] = y + _dot(pw_ref[dst, u], y.astype(BF16))
    for u in range(U):
        sol = sol_ref[u]
        u_ref[u] = sol[:, :HD]
        wd_ref[u] = sol[:, HD:].astype(BF16)

    for c in range(P // C):
        rows = slice(c * C, (c + 1) * C)
        states = [state_ref[u] for u in range(U)]
        sbf = [s.astype(BF16) for s in states]
        v_new = [(u_ref[u, rows, :] - _dot(wd_ref[u, rows, :], sbf[u])).astype(BF16) for u in range(U)]
        for u in range(U):
            g, h = divmod(u, H)
            oc_ref[g, rows, h * HD:(h + 1) * HD] = (
                _dot(qd_ref[u, rows, :], sbf[u]) + _dot(qkd_ref[u, rows, rows], v_new[u]))
        for u in range(U):
            g, h = divmod(u, H)
            g_tot = jnp.exp(gcT[g][h:h + 1, (c + 1) * C - 1:(c + 1) * C])
            state_ref[u] = states[u] * g_tot + _dot(kdT_ref[u, :, rows], v_new[u])

    for g in range(G):
        o = oc_ref[g]
        ms = head_sum(o * o) * (1.0 / HD)
        zz = z_ref[g]
        o_ref[g] = (o * lax.rsqrt(ms + EPS) * dng_ref[...] * (zz * _sigmoid(zz))).astype(BF16)


def _gdn(qkva, z, small, convw, alog_row, dtb_row, alog_col, dtb_col, dng, bd, *, batch, seq, G, P, C):
    W = GROUP_W
    H, HD = N_HEADS, HEAD_DIM
    U = G * H
    per_seq = lambda a: a.reshape(batch, seq, a.shape[-1])
    row = lambda b, i: (b, i, 0)
    const = lambda b, i: (0, 0)
    out = pl.pallas_call(
        functools.partial(_gdn_kernel, G=G, P=P, C=C),
        grid=(batch // G, seq // P),
        in_specs=[
            pl.BlockSpec((G, P, 3 * W), row),
            pl.BlockSpec((G, P, W), row),
            pl.BlockSpec((G, P, LANES), row),
            pl.BlockSpec((CONV_TAPS, 3 * W), const),
            pl.BlockSpec((1, LANES), const),
            pl.BlockSpec((1, LANES), const),
            pl.BlockSpec((H, 1), const),
            pl.BlockSpec((H, 1), const),
            pl.BlockSpec((1, W), const),
            pl.BlockSpec((W, W), const),
        ],
        out_specs=pl.BlockSpec((G, P, W), row),
        out_shape=jax.ShapeDtypeStruct((batch, seq, W), BF16),
        scratch_shapes=[
            pltpu.VMEM((G, P + 8, 3 * W), F32),
            pltpu.VMEM((U, P, HD), F32),
            pltpu.VMEM((U, P, HD), F32),
            pltpu.VMEM((U, P, HD), F32),
            pltpu.VMEM((2, U, P, P), BF16),
            pltpu.VMEM((U, P, P), BF16),
            pltpu.VMEM((U, P, 2 * HD), F32),
            pltpu.VMEM((U, P, HD), F32),
            pltpu.VMEM((U, P, HD), BF16),
            pltpu.VMEM((U, P, HD), BF16),
            pltpu.VMEM((U, HD, P), BF16),
            pltpu.VMEM((G, P, W), F32),
            pltpu.VMEM((U, HD, HD), F32),
        ],
        compiler_params=pltpu.CompilerParams(
            dimension_semantics=("parallel", "arbitrary"), vmem_limit_bytes=VMEM_LIMIT),
        name="gdn",
    )(per_seq(qkva), per_seq(z), per_seq(small), convw, alog_row, dtb_row, alog_col, dtb_col, dng, bd)
    return out.reshape(batch * seq, W)


def _key_to_f32(key):
    bits = key ^ (lax.shift_right_arithmetic(key, 31) & 0x7FFFFFFF)
    return pltpu.bitcast(bits, F32)


def _f32_to_key(x):
    bits = pltpu.bitcast(x, I32)
    return bits ^ (lax.shift_right_arithmetic(bits, 31) & 0x7FFFFFFF)


def _dsa_kernel(qb_ref, qi_ref, wsm_ref, kb_ref, vb_ref, ksm_ref, o_ref,
                sc_ref, vT_ref, kib_ref, s_ref, acc_ref, knorm_ref, *, Tq, Tk, seq, topk):
    HD = HEAD_DIM
    H = N_HEADS
    W = GROUP_W
    ACC = 2 * SUBLANES
    i = pl.program_id(1)
    nkt = ((i + 1) * Tq + Tk - 1) // Tk
    qpos = i * Tq + lax.broadcasted_iota(I32, (1, Tq), 1)
    sub_pos = lax.broadcasted_iota(I32, (Tk, Tq), 0)
    head_rows = lax.broadcasted_iota(I32, (H, W), 0)
    head_of_lane = lax.shift_right_logical(lax.broadcasted_iota(I32, (H, W), 1), int(math.log2(HD)))
    head_sel = jnp.where(head_rows == head_of_lane, 1.0, 0.0).astype(BF16)

    @pl.when(i == 0)
    def _():
        k_norm2 = jnp.zeros((H, 1), F32)
        for kt in range(seq // Tk):
            rows = slice(kt * Tk, (kt + 1) * Tk)
            vT_ref[kt] = vb_ref[rows, :].astype(F32).T.astype(BF16)
            k_tile = kb_ref[rows, :].astype(F32)
            per_head = _dot_nt(head_sel, (k_tile * k_tile).astype(BF16))
            k_norm2 = jnp.maximum(k_norm2, jnp.max(per_head, axis=1, keepdims=True))
        knorm_ref[...] = jnp.broadcast_to(jnp.sqrt(k_norm2), knorm_ref.shape)
        kib_ref[...] = ksm_ref[...].astype(BF16)

    wT = wsm_ref[...].T
    qi_heads = [qi_ref[:, h * HD:(h + 1) * HD] for h in range(H)]
    w_rows = [wT[SM_WI + h:SM_WI + h + 1, :] for h in range(H)]

    def fold(m):
        return jnp.sum(m.reshape(m.shape[0] // ACC, ACC, Tq), axis=0)

    def score_body(kt, carry):
        rmax, rmin, ge0, gt0 = carry
        r0 = pl.multiple_of(kt * Tk, Tk)
        ki = kib_ref[pl.ds(r0, Tk), :][:, SM_KI:SM_KI + HD]
        dots = [_dot_nt(ki, qi_heads[h]) for h in range(H)]
        score = jnp.zeros((Tk, Tq), F32)
        for h in range(H):
            score = score + w_rows[h] * jnp.maximum(dots[h], 0.0)
        adm = r0 + sub_pos <= qpos
        sc = jnp.where(adm, score, -jnp.inf)
        sc_ref[pl.ds(r0, Tk), :] = sc
        rmax = jnp.maximum(rmax, jnp.max(sc, axis=0, keepdims=True))
        rmin = jnp.minimum(rmin, jnp.min(jnp.where(adm, score, jnp.inf), axis=0, keepdims=True))
        ge0 = ge0 + fold(jnp.where(sc >= 0.0, 1.0, 0.0))
        gt0 = gt0 + fold(jnp.where(sc > 0.0, 1.0, 0.0))
        return rmax, rmin, ge0, gt0

    row = lambda v: jnp.full((1, Tq), v, F32)
    rmax, rmin, ge0, gt0 = lax.fori_loop(
        0, nkt, score_body, (row(-jnp.inf), row(jnp.inf), jnp.zeros((ACC, Tq), F32), jnp.zeros((ACC, Tq), F32)))
    cnt_ge0 = jnp.sum(ge0, axis=0, keepdims=True)
    cnt_gt0 = jnp.sum(gt0, axis=0, keepdims=True)

    def count(pred):
        def body(j, acc):
            r0 = pl.multiple_of(j * COUNT_ROWS, COUNT_ROWS)
            return acc + fold(pred(r0, sc_ref[pl.ds(r0, COUNT_ROWS), :]))
        acc = lax.fori_loop(0, nkt * (Tk // COUNT_ROWS), body, jnp.zeros((ACC, Tq), F32))
        return jnp.sum(acc, axis=0, keepdims=True)

    def count_ge(v):
        return count(lambda r0, s: jnp.where(s >= v, 1.0, 0.0))

    def count_gt(v):
        return count(lambda r0, s: jnp.where(s > v, 1.0, 0.0))

    kf = float(topk)
    log_k = math.log(kf)
    short = qpos < topk
    zero_tie = jnp.logical_and(cnt_gt0 < kf, cnt_ge0 >= kf)
    positive = cnt_gt0 >= kf
    tiny = row(FLT_MIN_NORMAL)
    lo_k = jnp.where(positive, _f32_to_key(tiny), _f32_to_key(rmin))
    hi_k = jnp.where(positive, _f32_to_key(rmax) + 1, _f32_to_key(-tiny))
    c_lo = jnp.where(positive, cnt_gt0, (qpos + 1).astype(F32))
    c_hi = jnp.where(positive, 0.0, cnt_ge0)
    done0 = jnp.where(jnp.logical_or(jnp.logical_or(short, zero_tie), c_lo == kf), 1.0, 0.0)

    def probe(st):
        it, lo_k, hi_k, c_lo, c_hi, w_lo, w_hi, last, done = st
        lo = _key_to_f32(lo_k)
        hi = _key_to_f32(hi_k - 1)
        a = (jnp.log(c_lo + 0.5) - log_k) * w_lo
        b = (log_k - jnp.log(c_hi + 0.5)) * w_hi
        frac = jnp.clip(a / jnp.maximum(a + b, 1e-30), 0.02, 0.98)
        cand_k = _f32_to_key(lo + frac * (hi - lo))
        mid_k = lo_k + lax.shift_right_logical(hi_k - lo_k, 1)
        cand_k = jnp.where(jnp.logical_and(it >= SEARCH_FAST_PASSES, it % 4 == 3), mid_k, cand_k)
        cand_k = jnp.minimum(jnp.maximum(cand_k, lo_k + 1), hi_k - 1)
        c = count_ge(_key_to_f32(cand_k))
        live = done == 0.0
        up = jnp.logical_and(live, c >= kf)
        dn = jnp.logical_and(live, c < kf)
        w_hi = jnp.where(jnp.logical_and(up, last > 0.0), w_hi * 0.5, jnp.where(dn, 1.0, w_hi))
        w_lo = jnp.where(jnp.logical_and(dn, last < 0.0), w_lo * 0.5, jnp.where(up, 1.0, w_lo))
        last = jnp.where(up, 1.0, jnp.where(dn, -1.0, last))
        lo_k = jnp.where(up, cand_k, lo_k)
        c_lo = jnp.where(up, c, c_lo)
        hi_k = jnp.where(dn, cand_k, hi_k)
        c_hi = jnp.where(dn, c, c_hi)
        done = jnp.where(jnp.logical_or(c_lo == kf, hi_k - lo_k <= 1), 1.0, done)
        return it + 1, lo_k, hi_k, c_lo, c_hi, w_lo, w_hi, last, done

    def search_cond(st):
        far = jnp.where(jnp.logical_and(st[-1] == 0.0, kf - st[4] > 1.0), 1.0, 0.0)
        return jnp.logical_and(st[0] < SEARCH_MAX_PASSES, jnp.max(far) > 0.0)

    st = (jnp.int32(0), lo_k, hi_k, c_lo, c_hi, row(1.0), row(1.0), row(0.0), done0)
    st = lax.while_loop(search_cond, lambda st: probe(probe(st)), st)
    _, lo_k, hi_k, c_lo, c_hi = st[:5]
    open_rows = st[-1] == 0.0

    lo_f = jnp.where(short, -jnp.inf, jnp.where(zero_tie, 0.0, _key_to_f32(lo_k)))
    hi_f = jnp.where(zero_tie, tiny, _key_to_f32(hi_k))
    c_lo = jnp.where(zero_tie, cnt_ge0, c_lo)
    c_hi = jnp.where(zero_tie, cnt_gt0, c_hi)

    def below_body(kt, best):
        r0 = pl.multiple_of(kt * Tk, Tk)
        s = sc_ref[pl.ds(r0, Tk), :]
        cand = jnp.where(s < hi_f, s, -jnp.inf)
        return jnp.maximum(best, jnp.max(cand.reshape(Tk // ACC, ACC, Tq), axis=0))

    below = jnp.max(lax.fori_loop(0, nkt, below_body, jnp.full((ACC, Tq), -jnp.inf, F32)),
                    axis=0, keepdims=True)
    lo_f = jnp.where(open_rows, below, lo_f)
    c_lo = jnp.where(open_rows, c_hi + 1.0, c_lo)
    cnt_gt = count_gt(lo_f)

    def unresolved(lo_f, hi_f, c_lo, c_hi, cnt_gt):
        inside = c_lo - c_hi
        u = jnp.logical_and(kf - c_hi < inside, c_lo - cnt_gt < inside)
        return jnp.where(short, 0.0, jnp.where(u, 1.0, 0.0))

    def refine_cond(st):
        return jnp.logical_and(st[0] < SEARCH_MAX_PASSES, jnp.max(unresolved(*st[1:])) > 0.0)

    def refine(st):
        it, lo_f, hi_f, c_lo, c_hi, cnt_gt = st
        live = unresolved(lo_f, hi_f, c_lo, c_hi, cnt_gt) > 0.0
        mid = 0.5 * lo_f + 0.5 * hi_f
        c = count_ge(mid)
        up = jnp.logical_and(live, c >= kf)
        dn = jnp.logical_and(live, c < kf)
        lo_f = jnp.where(up, mid, lo_f)
        c_lo = jnp.where(up, c, c_lo)
        hi_f = jnp.where(dn, mid, hi_f)
        c_hi = jnp.where(dn, c, c_hi)
        return it + 1, lo_f, hi_f, c_lo, c_hi, count_gt(lo_f)

    _, thr, _, c_thr, _, cnt_gt = lax.while_loop(
        refine_cond, refine, (jnp.int32(0), lo_f, hi_f, c_lo, c_hi, cnt_gt))

    quota = jnp.where(short, 0.0, kf - cnt_gt)
    before = jnp.where(lax.broadcasted_iota(I32, (Tk, Tk), 0) > lax.broadcasted_iota(I32, (Tk, Tk), 1),
                       1.0, 0.0).astype(BF16)

    def mask_body(kt, seen):
        r0 = pl.multiple_of(kt * Tk, Tk)
        sc = sc_ref[pl.ds(r0, Tk), :]
        eq = jnp.where(sc == thr, 1.0, 0.0)
        rank = _dot(before, eq.astype(BF16)) + seen
        tie = jnp.where(rank < quota, 0.0, -jnp.inf)
        sc_ref[pl.ds(r0, Tk), :] = jnp.where(sc > thr, 0.0, jnp.where(sc == thr, tie, -jnp.inf))
        return seen + jnp.sum(eq, axis=0, keepdims=True)

    lax.fori_loop(0, nkt, mask_body, jnp.zeros((1, Tq), F32))

    q_all = qb_ref[...].astype(F32)
    q_norm = jnp.sqrt(_dot_nt(head_sel, (q_all * q_all).astype(BF16)))
    shift = [q_norm[h:h + 1, :] * (ATT_BOUND_MARGIN * knorm_ref[h:h + 1, 0:1]) for h in range(H)]

    def att_fast(kt, l_all):
        r0 = pl.multiple_of(kt * Tk, Tk)
        bias = sc_ref[pl.ds(r0, Tk), :]
        for h in range(H):
            sl = slice(h * HD, (h + 1) * HD)
            s_ref[h] = _dot_nt(kb_ref[pl.ds(r0, Tk), sl], qb_ref[:, sl]) + bias - shift[h]
        l_rows = []
        for h in range(H):
            p = jnp.exp2(s_ref[h])
            l_rows.append(l_all[h:h + 1, :] + jnp.sum(p, axis=0, keepdims=True))
            acc_ref[h] += _dot(vT_ref[kt, h * HD:(h + 1) * HD, :], p.astype(BF16))
        return jnp.concatenate(l_rows, axis=0)

    acc_ref[...] = jnp.zeros_like(acc_ref)
    l_fast = lax.fori_loop(0, nkt, att_fast, jnp.zeros((H, Tq), F32))

    def att_slow(kt, carry):
        m_all, l_all = carry
        r0 = pl.multiple_of(kt * Tk, Tk)
        bias = sc_ref[pl.ds(r0, Tk), :]
        tile_max = []
        for h in range(H):
            sl = slice(h * HD, (h + 1) * HD)
            s = _dot_nt(kb_ref[pl.ds(r0, Tk), sl], qb_ref[:, sl]) + bias
            s_ref[h] = s
            tile_max.append(jnp.max(s, axis=0, keepdims=True))
        m_rows, l_rows = [], []
        for h in range(H):
            s = s_ref[h]
            m_old = m_all[h:h + 1, :]
            m_new = jnp.maximum(m_old, tile_max[h])
            alpha = jnp.exp2(m_old - m_new)
            p = jnp.exp2(s - m_new)
            l_rows.append(alpha * l_all[h:h + 1, :] + jnp.sum(p, axis=0, keepdims=True))
            acc_ref[h] = alpha * acc_ref[h] + _dot(vT_ref[kt, h * HD:(h + 1) * HD, :], p.astype(BF16))
            m_rows.append(m_new)
        return jnp.concatenate(m_rows, axis=0), jnp.concatenate(l_rows, axis=0)

    def redo():
        acc_ref[...] = jnp.zeros_like(acc_ref)
        init = (jnp.full((H, Tq), -1e30, F32), jnp.zeros((H, Tq), F32))
        return lax.fori_loop(0, nkt, att_slow, init)[1]

    l_fin = lax.cond(jnp.min(l_fast) > ATT_MIN_DENOM, lambda: l_fast, redo)
    out_t = jnp.concatenate([acc_ref[h] / l_fin[h:h + 1, :] for h in range(H)], axis=0)
    o_ref[...] = out_t.T.astype(BF16)


def _dsa(qb, kb, vb, qi, small, *, batch, seq, Tq, Tk):
    T = qb.shape[0]
    W = GROUP_W
    nq = seq // Tq
    qrow = lambda b, i: (b * nq + i, 0)
    full = lambda b, i: (b, 0)
    topk = min(TOPK_MAX, seq // 4)
    return pl.pallas_call(
        functools.partial(_dsa_kernel, Tq=Tq, Tk=Tk, seq=seq, topk=topk),
        grid=(batch, nq),
        in_specs=[
            pl.BlockSpec((Tq, W), qrow),
            pl.BlockSpec((Tq, W), qrow),
            pl.BlockSpec((Tq, LANES), qrow),
            pl.BlockSpec((seq, W), full),
            pl.BlockSpec((seq, W), full),
            pl.BlockSpec((seq, LANES), full),
        ],
        out_specs=pl.BlockSpec((Tq, W), qrow),
        out_shape=jax.ShapeDtypeStruct((T, W), BF16),
        scratch_shapes=[
            pltpu.VMEM((seq, Tq), F32),
            pltpu.VMEM((seq // Tk, W, Tk), BF16),
            pltpu.VMEM((seq, LANES), BF16),
            pltpu.VMEM((N_HEADS, Tk, Tq), F32),
            pltpu.VMEM((N_HEADS, HEAD_DIM, Tq), F32),
            pltpu.VMEM((N_HEADS, LANES), F32),
        ],
        compiler_params=pltpu.CompilerParams(
            dimension_semantics=("parallel", "arbitrary"), vmem_limit_bytes=VMEM_LIMIT),
        name="dsa",
    )(qb, qi, small, kb, vb, small)


def _merge_kernel(h_ref, oa_ref, ob_ref, g_ref, wa_ref, wb_ref, wo_ref, gpost_ref, o_ref):
    D = h_ref.shape[1]
    ya = _dot(oa_ref[...], wa_ref[...])
    yb = _dot(ob_ref[...], wb_ref[...])
    g = g_ref[...]
    merged = _sigmoid(g[:, :D]) * ya + _sigmoid(g[:, D:]) * yb
    mix = _dot(merged.astype(BF16), wo_ref[...])
    o_ref[...] = h_ref[...] + _rms(mix) * gpost_ref[...]


def _merge(h, oa, ob, g, wa, wb, wo, gpost, *, tm):
    T, D = h.shape
    W = GROUP_W
    row = lambda i: (i, 0)
    const = lambda i: (0, 0)
    return pl.pallas_call(
        _merge_kernel,
        grid=(T // tm,),
        in_specs=[
            pl.BlockSpec((tm, D), row),
            pl.BlockSpec((tm, W), row),
            pl.BlockSpec((tm, W), row),
            pl.BlockSpec((tm, 2 * D), row),
            pl.BlockSpec((W, D), const),
            pl.BlockSpec((W, D), const),
            pl.BlockSpec((D, D), const),
            pl.BlockSpec((1, D), const),
        ],
        out_specs=pl.BlockSpec((tm, D), row),
        out_shape=jax.ShapeDtypeStruct((T, D), F32),
        compiler_params=pltpu.CompilerParams(
            dimension_semantics=("parallel",), vmem_limit_bytes=VMEM_LIMIT),
        name="merge",
    )(h, oa, ob, g, wa, wb, wo, gpost)


def _ple_kernel(h_ref, p_ref, gpre_ref, wg_ref, wp_ref, gpost_ref, o_ref):
    h = h_ref[...]
    gate = _sigmoid(_dot((_rms(h) * gpre_ref[...]).astype(BF16), wg_ref[...]))
    e = _dot(p_ref[...].astype(BF16), wp_ref[...])
    o_ref[...] = h + _rms(gate * e) * gpost_ref[...]


def _ple(h, p, gpre, wg, wp, gpost, *, tm):
    T, D = h.shape
    P = p.shape[1]
    row = lambda i: (i, 0)
    const = lambda i: (0, 0)
    return pl.pallas_call(
        _ple_kernel,
        grid=(T // tm,),
        in_specs=[
            pl.BlockSpec((tm, D), row),
            pl.BlockSpec((tm, P), row),
            pl.BlockSpec((1, D), const),
            pl.BlockSpec((D, D), const),
            pl.BlockSpec((P, D), const),
            pl.BlockSpec((1, D), const),
        ],
        out_specs=pl.BlockSpec((tm, D), row),
        out_shape=jax.ShapeDtypeStruct((T, D), F32),
        compiler_params=pltpu.CompilerParams(
            dimension_semantics=("parallel",), vmem_limit_bytes=VMEM_LIMIT),
        name="ple",
    )(h, p, gpre, wg, wp, gpost)


def _rope_tables(seq):
    half = ROT_DIM // 2
    pos = jnp.arange(seq, dtype=F32)
    inv = ROPE_THETA ** (-jnp.arange(0, ROT_DIM, 2, dtype=F32) / ROT_DIM)
    ang = pos[:, None] * inv[None, :]
    cos, sin = jnp.cos(ang), jnp.sin(ang)
    ones = jnp.ones((seq, HEAD_DIM - ROT_DIM), F32)
    zeros_h = jnp.zeros((seq, half), F32)
    zeros_r = jnp.zeros((seq, HEAD_DIM - ROT_DIM), F32)
    c = jnp.concatenate([cos, cos, ones], axis=1)
    sa = jnp.concatenate([-sin, zeros_h, zeros_r], axis=1)
    sb = jnp.concatenate([zeros_h, sin, zeros_r], axis=1)
    rep = LANES // HEAD_DIM
    return jnp.tile(c, (1, rep)), jnp.tile(sa, (1, rep)), jnp.tile(sb, (1, rep))


def _pad_lanes(vec, offset):
    out = jnp.zeros((1, LANES), F32)
    return lax.dynamic_update_slice(out, vec.reshape(1, -1).astype(F32), (0, offset))


def kernel(x, p, ffn1_norm_pre, ffn1_norm_post, ffn1_w_in, ffn1_w_out, mix_norm_pre, mix_norm_post, mix_w_in, conv_w, a_log, dt_bias, dn_norm_g, idx_k_norm_g, w_br_a, w_br_b, mix_w_out, ffn2_norm_pre, ffn2_norm_post, ffn2_w_in, ffn2_w_out, ple_norm_pre, ple_norm_post, ple_w_gate, ple_w_proj):
    B, S, D = x.shape
    T = B * S
    W = GROUP_W
    depth = ffn1_w_in.shape[0]

    tm_ffn = min(512, T)
    tm_mix = min(512, S)
    gdn_rows = min(128, S)
    gdn_seqs = 2 if B % 2 == 0 else 1
    gdn_chunk = min(64, gdn_rows)
    Tq = min(512, S)
    Tk = min(512, S)
    tm_post = min(512, T)

    rope_c, rope_sa, rope_sb = _rope_tables(S)
    ri = jnp.arange(W)
    bd = (ri[:, None] // HEAD_DIM == ri[None, :] // HEAD_DIM).astype(BF16)
    smscale = _pad_lanes(jnp.full((N_HEADS,), N_HEADS ** -0.5, F32), SM_WI)
    smscale = jnp.where(smscale == 0.0, 1.0, smscale)
    vec = lambda a: a.reshape(1, -1).astype(F32)

    h = x.reshape(T, D)
    for l in range(depth):
        h = _ffn(h, vec(ffn1_norm_pre[l]), ffn1_w_in[l].astype(BF16), ffn1_w_out[l].astype(BF16),
                 vec(ffn1_norm_post[l]), tm=tm_ffn, n_split=2)

        wi = mix_w_in[l]
        o_a, o_b = 4 * W, 4 * W + 2 * N_HEADS
        o_ki = o_b + 4 * W
        o_wi = o_ki + HEAD_DIM
        o_g = o_wi + N_HEADS
        w_cat = jnp.concatenate([
            wi[:, :4 * W], wi[:, o_b:o_ki], wi[:, o_g:o_g + 2 * D],
            wi[:, o_ki:o_g], wi[:, o_a:o_b],
            jnp.zeros((D, LANES - HEAD_DIM - 3 * N_HEADS), wi.dtype)], axis=1).astype(BF16)
        qkva, z, qb, kb, vb, qi, g, small = _mix_in(
            h, vec(mix_norm_pre[l]), w_cat, rope_c, rope_sa, rope_sb,
            _pad_lanes(idx_k_norm_g[l], SM_KI), smscale, tm=tm_mix, seq=S)

        o_gdn = _gdn(qkva, z, small, conv_w[l].astype(F32),
                     _pad_lanes(a_log[l], SM_A), _pad_lanes(dt_bias[l], SM_A),
                     a_log[l].reshape(N_HEADS, 1).astype(F32), dt_bias[l].reshape(N_HEADS, 1).astype(F32),
                     jnp.tile(vec(dn_norm_g[l]), (1, N_HEADS)), bd,
                     batch=B, seq=S, G=gdn_seqs, P=gdn_rows, C=gdn_chunk)
        o_dsa = _dsa(qb, kb, vb, qi, small, batch=B, seq=S, Tq=Tq, Tk=Tk)

        h = _merge(h, o_gdn, o_dsa, g, w_br_a[l].astype(BF16), w_br_b[l].astype(BF16),
                   mix_w_out[l].astype(BF16), vec(mix_norm_post[l]), tm=tm_post)
        h = _ffn(h, vec(ffn2_norm_pre[l]), ffn2_w_in[l].astype(BF16), ffn2_w_out[l].astype(BF16),
                 vec(ffn2_norm_post[l]), tm=tm_ffn, n_split=2)
        h = _ple(h, p[l].reshape(T, -1), vec(ple_norm_pre[l]), ple_w_gate[l].astype(BF16),
                 ple_w_proj[l].astype(BF16), vec(ple_norm_post[l]), tm=tm_post)
    return h.reshape(B, S, D)
```

```python
import functools
import math

import jax
import jax.numpy as jnp
from jax import lax
from jax.experimental import pallas as pl
from jax.experimental.pallas import tpu as pltpu

F32 = jnp.float32
BF16 = jnp.bfloat16
I32 = jnp.int32

EPS = 1e-6
N_HEADS = 8
HEAD_DIM = 64
GROUP_W = N_HEADS * HEAD_DIM
CONV_TAPS = 4
ROT_DIM = HEAD_DIM // 4
ROPE_THETA = 500000.0
TOPK_MAX = 256
LANES = 128
SUBLANES = 8
MXU_DEPTH = 256
FLT_MIN_NORMAL = 1.1754943508222875e-38
LOG2_E = 1.4426950408889634
SEARCH_FAST_PASSES = 24
SEARCH_MAX_PASSES = 256
COUNT_ROWS = 128
ATT_BOUND_MARGIN = 1.02
ATT_MIN_DENOM = 1e-30
VMEM_LIMIT = 56 * 1024 * 1024

SM_KI = 0
SM_WI = 64
SM_A = 72
SM_B = 80


def _rms(x):
    return x * lax.rsqrt(jnp.mean(x * x, axis=-1, keepdims=True) + EPS)


def _dot(a, b):
    return jnp.dot(a, b, preferred_element_type=F32)


def _dot_nt(a, b):
    return lax.dot_general(a, b, (((1,), (1,)), ((), ())), preferred_element_type=F32)


def _split3(x):
    hi = x.astype(BF16)
    r = x - hi.astype(F32)
    mid = r.astype(BF16)
    lo = (r - mid.astype(F32)).astype(BF16)
    return hi, mid, lo


def _softplus(x):
    return jnp.maximum(x, 0.0) + jnp.log(1.0 + jnp.exp(-jnp.abs(x)))


def _sigmoid(x):
    return 1.0 / (1.0 + jnp.exp(-x))


def _ffn_kernel(x_ref, gpre_ref, w_in_ref, w_out_ref, gpost_ref, *rest, splits, with_ple):
    o_ref = rest[-1]
    ff = w_out_ref.shape[0]
    x = x_ref[...]
    xn = (_rms(x) * gpre_ref[...]).astype(BF16)
    acc = None
    for lo, hi in splits:
        g = _dot(xn, w_in_ref[:, lo:hi])
        u = _dot(xn, w_in_ref[:, ff + lo:ff + hi])
        part = _dot((g * _sigmoid(g) * u).astype(BF16), w_out_ref[lo:hi, :])
        acc = part if acc is None else acc + part
    h = x + 0.5 * (_rms(acc) * gpost_ref[...])
    if with_ple:
        p_ref, pre_ref, wg_ref, wp_ref, post_ref = rest[:-1]
        gate = _sigmoid(_dot((_rms(h) * pre_ref[...]).astype(BF16), wg_ref[...]))
        e = _dot(p_ref[...].astype(BF16), wp_ref[...])
        h = h + _rms(gate * e) * post_ref[...]
    o_ref[...] = h


def _ffn(h, gpre, w_in, w_out, gpost, ple=None, *, tm, n_split):
    T, D = h.shape
    FF = w_out.shape[0]
    step = -(-FF // (n_split * MXU_DEPTH)) * MXU_DEPTH
    splits = tuple((lo, min(lo + step, FF)) for lo in range(0, FF, step))
    resident = dict(pipeline_mode=pl.Buffered(1))
    row = lambda i: (i, 0)
    const = lambda i: (0, 0)
    in_specs = [
        pl.BlockSpec((tm, D), row),
        pl.BlockSpec((1, D), const),
        pl.BlockSpec((D, 2 * FF), const, **resident),
        pl.BlockSpec((FF, D), const, **resident),
        pl.BlockSpec((1, D), const),
    ]
    args = [h, gpre, w_in, w_out, gpost]
    if ple is not None:
        p, _, w_gate, w_proj, _ = ple
        in_specs += [
            pl.BlockSpec((tm, p.shape[1]), row),
            pl.BlockSpec((1, D), const),
            pl.BlockSpec(w_gate.shape, const, **resident),
            pl.BlockSpec(w_proj.shape, const, **resident),
            pl.BlockSpec((1, D), const),
        ]
        args += list(ple)
    return pl.pallas_call(
        functools.partial(_ffn_kernel, splits=splits, with_ple=ple is not None),
        grid=(T // tm,),
        in_specs=in_specs,
        out_specs=pl.BlockSpec((tm, D), row),
        out_shape=jax.ShapeDtypeStruct((T, D), F32),
        compiler_params=pltpu.CompilerParams(
            dimension_semantics=("parallel",), vmem_limit_bytes=VMEM_LIMIT),
        name="ffn_ple" if ple is not None else "ffn",
    )(*args)


def _rope(x, c, sa, sb):
    half = ROT_DIM // 2
    return x * c + pltpu.roll(x, LANES - half, 1) * sa + pltpu.roll(x, half, 1) * sb


def _mix_in_kernel(h_ref, gpre_ref, w_ref, c_ref, sa_ref, sb_ref, kig_ref, smscale_ref,
                   qkva_ref, z_ref, qb_ref, kb_ref, vb_ref, qi_ref, g_ref, small_ref):
    xn = (_rms(h_ref[...]) * gpre_ref[...]).astype(BF16)
    c, sa, sb = c_ref[...], sa_ref[...], sb_ref[...]
    W = GROUP_W

    def proj(lo, hi):
        return _dot(xn, w_ref[:, lo:hi])

    qkva_ref[...] = proj(0, 3 * W)
    z_ref[...] = proj(3 * W, 4 * W)

    def roped(lo, scale):
        y = proj(lo, lo + W)
        parts = [_rope(y[:, t * LANES:(t + 1) * LANES], c, sa, sb) for t in range(W // LANES)]
        return (jnp.concatenate(parts, axis=1) * scale).astype(BF16)

    inv_sqrt_d = HEAD_DIM ** -0.5
    qb_ref[...] = roped(4 * W, inv_sqrt_d * LOG2_E)
    kb_ref[...] = roped(5 * W, 1.0)
    vb_ref[...] = proj(6 * W, 7 * W).astype(BF16)
    qi_ref[...] = roped(7 * W, inv_sqrt_d)
    g_ref[...] = proj(8 * W, 12 * W).astype(BF16)

    sm = proj(12 * W, 12 * W + LANES)
    lane = lax.broadcasted_iota(I32, sm.shape, 1)
    is_ki = lane < HEAD_DIM
    ms = jnp.sum(jnp.where(is_ki, sm * sm, 0.0), axis=-1, keepdims=True) * (1.0 / HEAD_DIM)
    ki = _rope(sm * lax.rsqrt(ms + EPS) * kig_ref[...], c, sa, sb)
    small_ref[...] = jnp.where(is_ki, ki, sm * smscale_ref[...])


def _mix_in(h, gpre, w, rope_c, rope_sa, rope_sb, kig, smscale, *, tm, seq):
    T, D = h.shape
    W = GROUP_W
    ncol = w.shape[1]
    nseq = seq // tm
    row = lambda i: (i, 0)
    const = lambda i: (0, 0)
    pos = lambda i: (i % nseq, 0)
    outs = [(3 * W, F32), (W, F32), (W, BF16), (W, BF16), (W, BF16), (W, BF16), (4 * W, BF16), (LANES, F32)]
    return pl.pallas_call(
        _mix_in_kernel,
        grid=(T // tm,),
        in_specs=[
            pl.BlockSpec((tm, D), row),
            pl.BlockSpec((1, D), const),
            pl.BlockSpec((D, ncol), const, pipeline_mode=pl.Buffered(1)),
            pl.BlockSpec((tm, LANES), pos),
            pl.BlockSpec((tm, LANES), pos),
            pl.BlockSpec((tm, LANES), pos),
            pl.BlockSpec((1, LANES), const),
            pl.BlockSpec((1, LANES), const),
        ],
        out_specs=[pl.BlockSpec((tm, n), row) for n, _ in outs],
        out_shape=[jax.ShapeDtypeStruct((T, n), dt) for n, dt in outs],
        compiler_params=pltpu.CompilerParams(
            dimension_semantics=("parallel",), vmem_limit_bytes=VMEM_LIMIT),
        name="mix_in",
    )(h, gpre, w, rope_c, rope_sa, rope_sb, kig, smscale)


def _gdn_kernel(qkv_ref, z_ref, small_ref, convw_ref, alog_ref, dtb_ref, alogc_ref, dtbc_ref,
                dng_ref, bd_ref, o_ref,
                xs_ref, q3_ref, k3_ref, v3_ref, pw_ref, qkd_ref, sol_ref, u_ref, wd_ref,
                qd_ref, kdT_ref, oc_ref, state_ref, *, G, P, C):
    W = GROUP_W
    HD = HEAD_DIM
    H = N_HEADS
    U = G * H
    shift = int(math.log2(C))
    i = pl.program_id(1)

    @pl.when(i == 0)
    def _():
        for g in range(G):
            xs_ref[g, 0:8, :] = jnp.zeros((8, 3 * W), F32)
        state_ref[...] = jnp.zeros_like(state_ref)

    bd = bd_ref[...]

    def head_sum(x):
        return _dot(x.astype(BF16), bd)

    ri = lax.broadcasted_iota(I32, (P, P), 0)
    ci = lax.broadcasted_iota(I32, (P, P), 1)
    same = lax.shift_right_logical(ri, shift) == lax.shift_right_logical(ci, shift)
    incl = jnp.logical_and(same, ri >= ci)
    diag = ri == ci
    tri = jnp.where(incl, 1.0, 0.0).astype(BF16)
    triT = jnp.where(same, jnp.where(ci >= ri, 1.0, 0.0), 0.0).astype(BF16)
    blk = jnp.where(same, 1.0, 0.0).astype(BF16)

    gc, gcT, bfull, egc, ekd = [], [], [], [], []
    for g in range(G):
        xs_ref[g, 8:8 + P, :] = qkv_ref[g]
        w = convw_ref[...]
        y = w[CONV_TAPS - 1:CONV_TAPS, :] * xs_ref[g, 8:8 + P, :]
        for tap in range(CONV_TAPS - 1):
            off = 8 - (CONV_TAPS - 1) + tap
            y = y + w[tap:tap + 1, :] * xs_ref[g, off:off + P, :]
        xs_ref[g, 0:8, :] = xs_ref[g, P:P + 8, :]
        y = y * _sigmoid(y)
        q, k, v = y[:, :W], y[:, W:2 * W], y[:, 2 * W:]
        qn = q * lax.rsqrt(head_sum(q * q) + EPS) * (HD ** -0.5)
        kn = k * lax.rsqrt(head_sum(k * k) + EPS)
        for h in range(H):
            sl = slice(h * HD, (h + 1) * HD)
            q3_ref[g * H + h] = qn[:, sl]
            k3_ref[g * H + h] = kn[:, sl]
            v3_ref[g * H + h] = v[:, sl]

        sm = small_ref[g]
        gfull = -jnp.exp(alog_ref[...]) * _softplus(sm + dtb_ref[...])
        gT = -jnp.exp(alogc_ref[...]) * _softplus(sm.T[SM_A:SM_A + H, :] + dtbc_ref[...])
        g_parts = _split3(gfull)
        gc_g = sum(_dot(tri, part) for part in g_parts)
        gtot = sum(_dot(blk, part) for part in g_parts)
        gc.append(gc_g)
        gcT.append(sum(_dot(part, triT) for part in _split3(gT)))
        bfull.append(_sigmoid(sm))
        egc.append(jnp.exp(gc_g))
        ekd.append(jnp.exp(gtot - gc_g))

    for u in range(U):
        g, h = divmod(u, H)
        K, Q, V = k3_ref[u], q3_ref[u], v3_ref[u]
        Kb = K.astype(BF16)
        gcol = gc[g][:, SM_A + h:SM_A + h + 1]
        grow = gcT[g][h:h + 1, :]
        bcol = bfull[g][:, SM_B + h:SM_B + h + 1]
        ecol = egc[g][:, SM_A + h:SM_A + h + 1]
        decay = jnp.where(incl, jnp.exp(gcol - grow), 0.0)
        n_mat = jnp.where(diag, 0.0, -(bcol * _dot_nt(Kb, Kb) * decay)).astype(BF16)
        pw_ref[0, u] = n_mat
        qkd_ref[u] = (_dot_nt(Q.astype(BF16), Kb) * decay).astype(BF16)
        rhs = jnp.concatenate([V * bcol, K * (bcol * ecol)], axis=1)
        sol_ref[u] = rhs + _dot(n_mat, rhs.astype(BF16))
        qd_ref[u] = (Q * ecol).astype(BF16)
        kdT_ref[u] = (K * ekd[g][:, SM_A + h:SM_A + h + 1]).T.astype(BF16)

    for it in range(shift - 1):
        src, dst = it % 2, 1 - it % 2
        for u in range(U):
            pw = pw_ref[src, u]
            pw_ref[dst, u] = _dot(pw, pw).astype(BF16)
        for u in range(U):
            y = sol_ref[u]
            sol_ref[u] = y + _dot(pw_ref[dst, u], y.astype(BF16))
    for u in range(U):
        sol = sol_ref[u]
        u_ref[u] = sol[:, :HD]
        wd_ref[u] = sol[:, HD:].astype(BF16)

    for c in range(P // C):
        rows = slice(c * C, (c + 1) * C)
        states = [state_ref[u] for u in range(U)]
        sbf = [s.astype(BF16) for s in states]
        v_new = [(u_ref[u, rows, :] - _dot(wd_ref[u, rows, :], sbf[u])).astype(BF16) for u in range(U)]
        for u in range(U):
            g, h = divmod(u, H)
            oc_ref[g, rows, h * HD:(h + 1) * HD] = (
                _dot(qd_ref[u, rows, :], sbf[u]) + _dot(qkd_ref[u, rows, rows], v_new[u]))
        for u in range(U):
            g, h = divmod(u, H)
            g_tot = jnp.exp(gcT[g][h:h + 1, (c + 1) * C - 1:(c + 1) * C])
            state_ref[u] = states[u] * g_tot + _dot(kdT_ref[u, :, rows], v_new[u])

    for g in range(G):
        o = oc_ref[g]
        ms = head_sum(o * o) * (1.0 / HD)
        zz = z_ref[g]
        o_ref[g] = (o * lax.rsqrt(ms + EPS) * dng_ref[...] * (zz * _sigmoid(zz))).astype(BF16)


def _gdn(qkva, z, small, convw, alog_row, dtb_row, alog_col, dtb_col, dng, bd, *, batch, seq, G, P, C):
    W = GROUP_W
    H, HD = N_HEADS, HEAD_DIM
    U = G * H
    per_seq = lambda a: a.reshape(batch, seq, a.shape[-1])
    row = lambda b, i: (b, i, 0)
    const = lambda b, i: (0, 0)
    out = pl.pallas_call(
        functools.partial(_gdn_kernel, G=G, P=P, C=C),
        grid=(batch // G, seq // P),
        in_specs=[
            pl.BlockSpec((G, P, 3 * W), row),
            pl.BlockSpec((G, P, W), row),
            pl.BlockSpec((G, P, LANES), row),
            pl.BlockSpec((CONV_TAPS, 3 * W), const),
            pl.BlockSpec((1, LANES), const),
            pl.BlockSpec((1, LANES), const),
            pl.BlockSpec((H, 1), const),
            pl.BlockSpec((H, 1), const),
            pl.BlockSpec((1, W), const),
            pl.BlockSpec((W, W), const),
        ],
        out_specs=pl.BlockSpec((G, P, W), row),
        out_shape=jax.ShapeDtypeStruct((batch, seq, W), BF16),
        scratch_shapes=[
            pltpu.VMEM((G, P + 8, 3 * W), F32),
            pltpu.VMEM((U, P, HD), F32),
            pltpu.VMEM((U, P, HD), F32),
            pltpu.VMEM((U, P, HD), F32),
            pltpu.VMEM((2, U, P, P), BF16),
            pltpu.VMEM((U, P, P), BF16),
            pltpu.VMEM((U, P, 2 * HD), F32),
            pltpu.VMEM((U, P, HD), F32),
            pltpu.VMEM((U, P, HD), BF16),
            pltpu.VMEM((U, P, HD), BF16),
            pltpu.VMEM((U, HD, P), BF16),
            pltpu.VMEM((G, P, W), F32),
            pltpu.VMEM((U, HD, HD), F32),
        ],
        compiler_params=pltpu.CompilerParams(
            dimension_semantics=("parallel", "arbitrary"), vmem_limit_bytes=VMEM_LIMIT),
        name="gdn",
    )(per_seq(qkva), per_seq(z), per_seq(small), convw, alog_row, dtb_row, alog_col, dtb_col, dng, bd)
    return out.reshape(batch * seq, W)


def _key_to_f32(key):
    bits = key ^ (lax.shift_right_arithmetic(key, 31) & 0x7FFFFFFF)
    return pltpu.bitcast(bits, F32)


def _f32_to_key(x):
    bits = pltpu.bitcast(x, I32)
    return bits ^ (lax.shift_right_arithmetic(bits, 31) & 0x7FFFFFFF)


def _dsa_kernel(qb_ref, qi_ref, wsm_ref, kb_ref, vb_ref, ksm_ref, o_ref,
                sc_ref, vT_ref, kib_ref, s_ref, acc_ref, knorm_ref, *, Tq, Tk, seq, topk):
    HD = HEAD_DIM
    H = N_HEADS
    W = GROUP_W
    ACC = 2 * SUBLANES
    i = pl.program_id(1)
    nkt = ((i + 1) * Tq + Tk - 1) // Tk
    qpos = i * Tq + lax.broadcasted_iota(I32, (1, Tq), 1)
    sub_pos = lax.broadcasted_iota(I32, (Tk, Tq), 0)
    head_rows = lax.broadcasted_iota(I32, (H, W), 0)
    head_of_lane = lax.shift_right_logical(lax.broadcasted_iota(I32, (H, W), 1), int(math.log2(HD)))
    head_sel = jnp.where(head_rows == head_of_lane, 1.0, 0.0).astype(BF16)

    @pl.when(i == 0)
    def _():
        k_norm2 = jnp.zeros((H, 1), F32)
        for kt in range(seq // Tk):
            rows = slice(kt * Tk, (kt + 1) * Tk)
            vT_ref[kt] = vb_ref[rows, :].astype(F32).T.astype(BF16)
            k_tile = kb_ref[rows, :].astype(F32)
            per_head = _dot_nt(head_sel, (k_tile * k_tile).astype(BF16))
            k_norm2 = jnp.maximum(k_norm2, jnp.max(per_head, axis=1, keepdims=True))
        knorm_ref[...] = jnp.broadcast_to(jnp.sqrt(k_norm2), knorm_ref.shape)
        kib_ref[...] = ksm_ref[...].astype(BF16)

    wT = wsm_ref[...].T
    qi_heads = [qi_ref[:, h * HD:(h + 1) * HD] for h in range(H)]
    w_rows = [wT[SM_WI + h:SM_WI + h + 1, :] for h in range(H)]

    def fold(m):
        return jnp.sum(m.reshape(m.shape[0] // ACC, ACC, Tq), axis=0)

    def score_body(kt, carry):
        rmax, rmin, ge0, gt0 = carry
        r0 = pl.multiple_of(kt * Tk, Tk)
        ki = kib_ref[pl.ds(r0, Tk), :][:, SM_KI:SM_KI + HD]
        dots = [_dot_nt(ki, qi_heads[h]) for h in range(H)]
        score = jnp.zeros((Tk, Tq), F32)
        for h in range(H):
            score = score + w_rows[h] * jnp.maximum(dots[h], 0.0)
        adm = r0 + sub_pos <= qpos
        sc = jnp.where(adm, score, -jnp.inf)
        sc_ref[pl.ds(r0, Tk), :] = sc
        rmax = jnp.maximum(rmax, jnp.max(sc, axis=0, keepdims=True))
        rmin = jnp.minimum(rmin, jnp.min(jnp.where(adm, score, jnp.inf), axis=0, keepdims=True))
        ge0 = ge0 + fold(jnp.where(sc >= 0.0, 1.0, 0.0))
        gt0 = gt0 + fold(jnp.where(sc > 0.0, 1.0, 0.0))
        return rmax, rmin, ge0, gt0

    row = lambda v: jnp.full((1, Tq), v, F32)
    rmax, rmin, ge0, gt0 = lax.fori_loop(
        0, nkt, score_body, (row(-jnp.inf), row(jnp.inf), jnp.zeros((ACC, Tq), F32), jnp.zeros((ACC, Tq), F32)))
    cnt_ge0 = jnp.sum(ge0, axis=0, keepdims=True)
    cnt_gt0 = jnp.sum(gt0, axis=0, keepdims=True)

    def count(pred):
        def body(j, acc):
            r0 = pl.multiple_of(j * COUNT_ROWS, COUNT_ROWS)
            return acc + fold(pred(r0, sc_ref[pl.ds(r0, COUNT_ROWS), :]))
        acc = lax.fori_loop(0, nkt * (Tk // COUNT_ROWS), body, jnp.zeros((ACC, Tq), F32))
        return jnp.sum(acc, axis=0, keepdims=True)

    def count_ge(v):
        return count(lambda r0, s: jnp.where(s >= v, 1.0, 0.0))

    def count_gt(v):
        return count(lambda r0, s: jnp.where(s > v, 1.0, 0.0))

    kf = float(topk)
    log_k = math.log(kf)
    short = qpos < topk
    zero_tie = jnp.logical_and(cnt_gt0 < kf, cnt_ge0 >= kf)
    positive = cnt_gt0 >= kf
    tiny = row(FLT_MIN_NORMAL)
    lo_k = jnp.where(positive, _f32_to_key(tiny), _f32_to_key(rmin))
    hi_k = jnp.where(positive, _f32_to_key(rmax) + 1, _f32_to_key(-tiny))
    c_lo = jnp.where(positive, cnt_gt0, (qpos + 1).astype(F32))
    c_hi = jnp.where(positive, 0.0, cnt_ge0)
    done0 = jnp.where(jnp.logical_or(jnp.logical_or(short, zero_tie), c_lo == kf), 1.0, 0.0)

    def probe(st):
        it, lo_k, hi_k, c_lo, c_hi, w_lo, w_hi, last, done = st
        lo = _key_to_f32(lo_k)
        hi = _key_to_f32(hi_k - 1)
        a = (jnp.log(c_lo + 0.5) - log_k) * w_lo
        b = (log_k - jnp.log(c_hi + 0.5)) * w_hi
        frac = jnp.clip(a / jnp.maximum(a + b, 1e-30), 0.02, 0.98)
        cand_k = _f32_to_key(lo + frac * (hi - lo))
        mid_k = lo_k + lax.shift_right_logical(hi_k - lo_k, 1)
        cand_k = jnp.where(jnp.logical_and(it >= SEARCH_FAST_PASSES, it % 4 == 3), mid_k, cand_k)
        cand_k = jnp.minimum(jnp.maximum(cand_k, lo_k + 1), hi_k - 1)
        c = count_ge(_key_to_f32(cand_k))
        live = done == 0.0
        up = jnp.logical_and(live, c >= kf)
        dn = jnp.logical_and(live, c < kf)
        w_hi = jnp.where(jnp.logical_and(up, last > 0.0), w_hi * 0.5, jnp.where(dn, 1.0, w_hi))
        w_lo = jnp.where(jnp.logical_and(dn, last < 0.0), w_lo * 0.5, jnp.where(up, 1.0, w_lo))
        last = jnp.where(up, 1.0, jnp.where(dn, -1.0, last))
        lo_k = jnp.where(up, cand_k, lo_k)
        c_lo = jnp.where(up, c, c_lo)
        hi_k = jnp.where(dn, cand_k, hi_k)
        c_hi = jnp.where(dn, c, c_hi)
        done = jnp.where(jnp.logical_or(c_lo == kf, hi_k - lo_k <= 1), 1.0, done)
        return it + 1, lo_k, hi_k, c_lo, c_hi, w_lo, w_hi, last, done

    def search_cond(st):
        far = jnp.where(jnp.logical_and(st[-1] == 0.0, kf - st[4] > 1.0), 1.0, 0.0)
        return jnp.logical_and(st[0] < SEARCH_MAX_PASSES, jnp.max(far) > 0.0)

    st = (jnp.int32(0), lo_k, hi_k, c_lo, c_hi, row(1.0), row(1.0), row(0.0), done0)
    st = lax.while_loop(search_cond, lambda st: probe(probe(st)), st)
    _, lo_k, hi_k, c_lo, c_hi = st[:5]
    open_rows = st[-1] == 0.0

    lo_f = jnp.where(short, -jnp.inf, jnp.where(zero_tie, 0.0, _key_to_f32(lo_k)))
    hi_f = jnp.where(zero_tie, tiny, _key_to_f32(hi_k))
    c_lo = jnp.where(zero_tie, cnt_ge0, c_lo)
    c_hi = jnp.where(zero_tie, cnt_gt0, c_hi)

    def below_body(kt, best):
        r0 = pl.multiple_of(kt * Tk, Tk)
        s = sc_ref[pl.ds(r0, Tk), :]
        cand = jnp.where(s < hi_f, s, -jnp.inf)
        return jnp.maximum(best, jnp.max(cand.reshape(Tk // ACC, ACC, Tq), axis=0))

    below = jnp.max(lax.fori_loop(0, nkt, below_body, jnp.full((ACC, Tq), -jnp.inf, F32)),
                    axis=0, keepdims=True)
    lo_f = jnp.where(open_rows, below, lo_f)
    c_lo = jnp.where(open_rows, c_hi + 1.0, c_lo)
    cnt_gt = count_gt(lo_f)

    def unresolved(lo_f, hi_f, c_lo, c_hi, cnt_gt):
        inside = c_lo - c_hi
        u = jnp.logical_and(kf - c_hi < inside, c_lo - cnt_gt < inside)
        return jnp.where(short, 0.0, jnp.where(u, 1.0, 0.0))

    def refine_cond(st):
        return jnp.logical_and(st[0] < SEARCH_MAX_PASSES, jnp.max(unresolved(*st[1:])) > 0.0)

    def refine(st):
        it, lo_f, hi_f, c_lo, c_hi, cnt_gt = st
        live = unresolved(lo_f, hi_f, c_lo, c_hi, cnt_gt) > 0.0
        mid = 0.5 * lo_f + 0.5 * hi_f
        c = count_ge(mid)
        up = jnp.logical_and(live, c >= kf)
        dn = jnp.logical_and(live, c < kf)
        lo_f = jnp.where(up, mid, lo_f)
        c_lo = jnp.where(up, c, c_lo)
        hi_f = jnp.where(dn, mid, hi_f)
        c_hi = jnp.where(dn, c, c_hi)
        return it + 1, lo_f, hi_f, c_lo, c_hi, count_gt(lo_f)

    _, thr, _, _, _, cnt_gt = lax.while_loop(
        refine_cond, refine, (jnp.int32(0), lo_f, hi_f, c_lo, c_hi, cnt_gt))

    quota = jnp.where(short, 0.0, kf - cnt_gt)
    before = jnp.where(lax.broadcasted_iota(I32, (Tk, Tk), 0) > lax.broadcasted_iota(I32, (Tk, Tk), 1),
                       1.0, 0.0).astype(BF16)

    def mask_body(kt, seen):
        r0 = pl.multiple_of(kt * Tk, Tk)
        sc = sc_ref[pl.ds(r0, Tk), :]
        eq = jnp.where(sc == thr, 1.0, 0.0)
        rank = _dot(before, eq.astype(BF16)) + seen
        tie = jnp.where(rank < quota, 0.0, -jnp.inf)
        sc_ref[pl.ds(r0, Tk), :] = jnp.where(sc > thr, 0.0, jnp.where(sc == thr, tie, -jnp.inf))
        return seen + jnp.sum(eq, axis=0, keepdims=True)

    lax.fori_loop(0, nkt, mask_body, jnp.zeros((1, Tq), F32))

    q_all = qb_ref[...].astype(F32)
    q_norm = jnp.sqrt(_dot_nt(head_sel, (q_all * q_all).astype(BF16)))
    shift = [q_norm[h:h + 1, :] * (ATT_BOUND_MARGIN * knorm_ref[h:h + 1, 0:1]) for h in range(H)]

    def att_fast(kt, l_all):
        r0 = pl.multiple_of(kt * Tk, Tk)
        bias = sc_ref[pl.ds(r0, Tk), :]
        for h in range(H):
            sl = slice(h * HD, (h + 1) * HD)
            s_ref[h] = _dot_nt(kb_ref[pl.ds(r0, Tk), sl], qb_ref[:, sl]) + bias - shift[h]
        l_rows = []
        for h in range(H):
            p = jnp.exp2(s_ref[h])
            l_rows.append(l_all[h:h + 1, :] + jnp.sum(p, axis=0, keepdims=True))
            acc_ref[h] += _dot(vT_ref[kt, h * HD:(h + 1) * HD, :], p.astype(BF16))
        return jnp.concatenate(l_rows, axis=0)

    acc_ref[...] = jnp.zeros_like(acc_ref)
    l_fast = lax.fori_loop(0, nkt, att_fast, jnp.zeros((H, Tq), F32))

    def att_slow(kt, carry):
        m_all, l_all = carry
        r0 = pl.multiple_of(kt * Tk, Tk)
        bias = sc_ref[pl.ds(r0, Tk), :]
        tile_max = []
        for h in range(H):
            sl = slice(h * HD, (h + 1) * HD)
            s = _dot_nt(kb_ref[pl.ds(r0, Tk), sl], qb_ref[:, sl]) + bias
            s_ref[h] = s
            tile_max.append(jnp.max(s, axis=0, keepdims=True))
        m_rows, l_rows = [], []
        for h in range(H):
            s = s_ref[h]
            m_old = m_all[h:h + 1, :]
            m_new = jnp.maximum(m_old, tile_max[h])
            alpha = jnp.exp2(m_old - m_new)
            p = jnp.exp2(s - m_new)
            l_rows.append(alpha * l_all[h:h + 1, :] + jnp.sum(p, axis=0, keepdims=True))
            acc_ref[h] = alpha * acc_ref[h] + _dot(vT_ref[kt, h * HD:(h + 1) * HD, :], p.astype(BF16))
            m_rows.append(m_new)
        return jnp.concatenate(m_rows, axis=0), jnp.concatenate(l_rows, axis=0)

    def redo():
        acc_ref[...] = jnp.zeros_like(acc_ref)
        init = (jnp.full((H, Tq), -1e30, F32), jnp.zeros((H, Tq), F32))
        return lax.fori_loop(0, nkt, att_slow, init)[1]

    l_fin = lax.cond(jnp.min(l_fast) > ATT_MIN_DENOM, lambda: l_fast, redo)
    out_t = jnp.concatenate([acc_ref[h] / l_fin[h:h + 1, :] for h in range(H)], axis=0)
    o_ref[...] = out_t.T.astype(BF16)


def _dsa(qb, kb, vb, qi, small, *, batch, seq, Tq, Tk):
    T = qb.shape[0]
    W = GROUP_W
    nq = seq // Tq
    qrow = lambda b, i: (b * nq + i, 0)
    full = lambda b, i: (b, 0)
    topk = min(TOPK_MAX, seq // 4)
    return pl.pallas_call(
        functools.partial(_dsa_kernel, Tq=Tq, Tk=Tk, seq=seq, topk=topk),
        grid=(batch, nq),
        in_specs=[
            pl.BlockSpec((Tq, W), qrow),
            pl.BlockSpec((Tq, W), qrow),
            pl.BlockSpec((Tq, LANES), qrow),
            pl.BlockSpec((seq, W), full),
            pl.BlockSpec((seq, W), full),
            pl.BlockSpec((seq, LANES), full),
        ],
        out_specs=pl.BlockSpec((Tq, W), qrow),
        out_shape=jax.ShapeDtypeStruct((T, W), BF16),
        scratch_shapes=[
            pltpu.VMEM((seq, Tq), F32),
            pltpu.VMEM((seq // Tk, W, Tk), BF16),
            pltpu.VMEM((seq, LANES), BF16),
            pltpu.VMEM((N_HEADS, Tk, Tq), F32),
            pltpu.VMEM((N_HEADS, HEAD_DIM, Tq), F32),
            pltpu.VMEM((N_HEADS, LANES), F32),
        ],
        compiler_params=pltpu.CompilerParams(
            dimension_semantics=("parallel", "arbitrary"), vmem_limit_bytes=VMEM_LIMIT),
        name="dsa",
    )(qb, qi, small, kb, vb, small)


def _merge_kernel(h_ref, oa_ref, ob_ref, g_ref, wa_ref, wb_ref, wo_ref, gpost_ref, o_ref):
    D = h_ref.shape[1]
    ya = _dot(oa_ref[...], wa_ref[...])
    yb = _dot(ob_ref[...], wb_ref[...])
    g = g_ref[...].astype(F32)
    merged = _sigmoid(g[:, :D]) * ya + _sigmoid(g[:, D:]) * yb
    mix = _dot(merged.astype(BF16), wo_ref[...])
    o_ref[...] = h_ref[...] + _rms(mix) * gpost_ref[...]


def _merge(h, oa, ob, g, wa, wb, wo, gpost, *, tm):
    T, D = h.shape
    W = GROUP_W
    row = lambda i: (i, 0)
    const = lambda i: (0, 0)
    return pl.pallas_call(
        _merge_kernel,
        grid=(T // tm,),
        in_specs=[
            pl.BlockSpec((tm, D), row),
            pl.BlockSpec((tm, W), row),
            pl.BlockSpec((tm, W), row),
            pl.BlockSpec((tm, 2 * D), row),
            pl.BlockSpec((W, D), const),
            pl.BlockSpec((W, D), const),
            pl.BlockSpec((D, D), const),
            pl.BlockSpec((1, D), const),
        ],
        out_specs=pl.BlockSpec((tm, D), row),
        out_shape=jax.ShapeDtypeStruct((T, D), F32),
        compiler_params=pltpu.CompilerParams(
            dimension_semantics=("parallel",), vmem_limit_bytes=VMEM_LIMIT),
        name="merge",
    )(h, oa, ob, g, wa, wb, wo, gpost)


def _rope_tables(seq):
    half = ROT_DIM // 2
    pos = jnp.arange(seq, dtype=F32)
    inv = ROPE_THETA ** (-jnp.arange(0, ROT_DIM, 2, dtype=F32) / ROT_DIM)
    ang = pos[:, None] * inv[None, :]
    cos, sin = jnp.cos(ang), jnp.sin(ang)
    ones = jnp.ones((seq, HEAD_DIM - ROT_DIM), F32)
    zeros_h = jnp.zeros((seq, half), F32)
    zeros_r = jnp.zeros((seq, HEAD_DIM - ROT_DIM), F32)
    c = jnp.concatenate([cos, cos, ones], axis=1)
    sa = jnp.concatenate([-sin, zeros_h, zeros_r], axis=1)
    sb = jnp.concatenate([zeros_h, sin, zeros_r], axis=1)
    rep = LANES // HEAD_DIM
    return jnp.tile(c, (1, rep)), jnp.tile(sa, (1, rep)), jnp.tile(sb, (1, rep))


def _pad_lanes(vec, offset):
    out = jnp.zeros((1, LANES), F32)
    return lax.dynamic_update_slice(out, vec.reshape(1, -1).astype(F32), (0, offset))


def kernel(x, p, ffn1_norm_pre, ffn1_norm_post, ffn1_w_in, ffn1_w_out, mix_norm_pre, mix_norm_post, mix_w_in, conv_w, a_log, dt_bias, dn_norm_g, idx_k_norm_g, w_br_a, w_br_b, mix_w_out, ffn2_norm_pre, ffn2_norm_post, ffn2_w_in, ffn2_w_out, ple_norm_pre, ple_norm_post, ple_w_gate, ple_w_proj):
    B, S, D = x.shape
    T = B * S
    W = GROUP_W
    depth = ffn1_w_in.shape[0]

    tm_ffn = min(512, T)
    tm_mix = min(512, S)
    gdn_rows = min(128, S)
    gdn_seqs = 2 if B % 2 == 0 else 1
    gdn_chunk = min(64, gdn_rows)
    Tq = min(512, S)
    Tk = min(512, S)
    tm_post = min(512, T)

    rope_c, rope_sa, rope_sb = _rope_tables(S)
    ri = jnp.arange(W)
    bd = (ri[:, None] // HEAD_DIM == ri[None, :] // HEAD_DIM).astype(BF16)
    smscale = _pad_lanes(jnp.full((N_HEADS,), N_HEADS ** -0.5, F32), SM_WI)
    smscale = jnp.where(smscale == 0.0, 1.0, smscale)
    vec = lambda a: a.reshape(1, -1).astype(F32)

    h = x.reshape(T, D)
    for l in range(depth):
        h = _ffn(h, vec(ffn1_norm_pre[l]), ffn1_w_in[l].astype(BF16), ffn1_w_out[l].astype(BF16),
                 vec(ffn1_norm_post[l]), tm=tm_ffn, n_split=2)

        wi = mix_w_in[l]
        o_a, o_b = 4 * W, 4 * W + 2 * N_HEADS
        o_ki = o_b + 4 * W
        o_wi = o_ki + HEAD_DIM
        o_g = o_wi + N_HEADS
        w_cat = jnp.concatenate([
            wi[:, :4 * W], wi[:, o_b:o_ki], wi[:, o_g:o_g + 2 * D],
            wi[:, o_ki:o_g], wi[:, o_a:o_b],
            jnp.zeros((D, LANES - HEAD_DIM - 3 * N_HEADS), wi.dtype)], axis=1).astype(BF16)
        qkva, z, qb, kb, vb, qi, g, small = _mix_in(
            h, vec(mix_norm_pre[l]), w_cat, rope_c, rope_sa, rope_sb,
            _pad_lanes(idx_k_norm_g[l], SM_KI), smscale, tm=tm_mix, seq=S)

        o_gdn = _gdn(qkva, z, small, conv_w[l].astype(F32),
                     _pad_lanes(a_log[l], SM_A), _pad_lanes(dt_bias[l], SM_A),
                     a_log[l].reshape(N_HEADS, 1).astype(F32), dt_bias[l].reshape(N_HEADS, 1).astype(F32),
                     jnp.tile(vec(dn_norm_g[l]), (1, N_HEADS)), bd,
                     batch=B, seq=S, G=gdn_seqs, P=gdn_rows, C=gdn_chunk)
        o_dsa = _dsa(qb, kb, vb, qi, small, batch=B, seq=S, Tq=Tq, Tk=Tk)

        h = _merge(h, o_gdn, o_dsa, g, w_br_a[l].astype(BF16), w_br_b[l].astype(BF16),
                   mix_w_out[l].astype(BF16), vec(mix_norm_post[l]), tm=tm_post)
        ple = (p[l].reshape(T, -1), vec(ple_norm_pre[l]), ple_w_gate[l].astype(BF16),
               ple_w_proj[l].astype(BF16), vec(ple_norm_post[l]))
        h = _ffn(h, vec(ffn2_norm_pre[l]), ffn2_w_in[l].astype(BF16), ffn2_w_out[l].astype(BF16),
                 vec(ffn2_norm_post[l]), ple, tm=tm_ffn, n_split=2)
    return h.reshape(B, S, D)
```

```python
import functools
import math

import jax
import jax.numpy as jnp
from jax import lax
from jax.experimental import pallas as pl
from jax.experimental.pallas import tpu as pltpu

F32 = jnp.float32
BF16 = jnp.bfloat16
I32 = jnp.int32

EPS = 1e-6
N_HEADS = 8
HEAD_DIM = 64
GROUP_W = N_HEADS * HEAD_DIM
CONV_TAPS = 4
ROT_DIM = HEAD_DIM // 4
ROPE_THETA = 500000.0
TOPK_MAX = 256
LANES = 128
SUBLANES = 8
MXU_DEPTH = 256
FLT_MIN_NORMAL = 1.1754943508222875e-38
LOG2_E = 1.4426950408889634
SEARCH_FAST_PASSES = 24
SEARCH_MAX_PASSES = 256
COUNT_ROWS = 128
ATT_BOUND_MARGIN = 1.02
ATT_MIN_DENOM = 1e-30
VMEM_LIMIT = 56 * 1024 * 1024

SM_KI = 0
SM_WI = 64
SM_A = 72
SM_B = 80


def _rms(x):
    return x * lax.rsqrt(jnp.mean(x * x, axis=-1, keepdims=True) + EPS)


def _dot(a, b):
    return jnp.dot(a, b, preferred_element_type=F32)


def _dot_nt(a, b):
    return lax.dot_general(a, b, (((1,), (1,)), ((), ())), preferred_element_type=F32)


def _split3(x):
    hi = x.astype(BF16)
    r = x - hi.astype(F32)
    mid = r.astype(BF16)
    lo = (r - mid.astype(F32)).astype(BF16)
    return hi, mid, lo


def _softplus(x):
    return jnp.maximum(x, 0.0) + jnp.log(1.0 + jnp.exp(-jnp.abs(x)))


def _sigmoid(x):
    return 1.0 / (1.0 + jnp.exp(-x))


def _ffn_kernel(x_ref, gpre_ref, w_in_ref, w_out_ref, gpost_ref, *rest, splits, with_mix, with_ple):
    o_ref = rest[-1]
    rest = rest[:-1]
    ff = w_out_ref.shape[0]
    x = x_ref[...]
    if with_mix:
        oa_ref, ob_ref, g_ref, wa_ref, wb_ref, wo_ref, mpost_ref = rest[:7]
        rest = rest[7:]
        D = x.shape[1]
        gates = g_ref[...].astype(F32)
        merged = (_sigmoid(gates[:, :D]) * _dot(oa_ref[...], wa_ref[...])
                  + _sigmoid(gates[:, D:]) * _dot(ob_ref[...], wb_ref[...]))
        x = x + _rms(_dot(merged.astype(BF16), wo_ref[...])) * mpost_ref[...]
    xn = (_rms(x) * gpre_ref[...]).astype(BF16)
    acc = None
    for lo, hi in splits:
        g = _dot(xn, w_in_ref[:, lo:hi])
        u = _dot(xn, w_in_ref[:, ff + lo:ff + hi])
        part = _dot((g * _sigmoid(g) * u).astype(BF16), w_out_ref[lo:hi, :])
        acc = part if acc is None else acc + part
    h = x + 0.5 * (_rms(acc) * gpost_ref[...])
    if with_ple:
        p_ref, pre_ref, wg_ref, wp_ref, post_ref = rest
        gate = _sigmoid(_dot((_rms(h) * pre_ref[...]).astype(BF16), wg_ref[...]))
        e = _dot(p_ref[...].astype(BF16), wp_ref[...])
        h = h + _rms(gate * e) * post_ref[...]
    o_ref[...] = h


def _ffn(h, gpre, w_in, w_out, gpost, mix=None, ple=None, *, tm, n_split):
    T, D = h.shape
    FF = w_out.shape[0]
    step = -(-FF // (n_split * MXU_DEPTH)) * MXU_DEPTH
    splits = tuple((lo, min(lo + step, FF)) for lo in range(0, FF, step))
    resident = dict(pipeline_mode=pl.Buffered(1))
    row = lambda i: (i, 0)
    const = lambda i: (0, 0)
    in_specs = [
        pl.BlockSpec((tm, D), row),
        pl.BlockSpec((1, D), const),
        pl.BlockSpec((D, 2 * FF), const, **resident),
        pl.BlockSpec((FF, D), const, **resident),
        pl.BlockSpec((1, D), const),
    ]
    args = [h, gpre, w_in, w_out, gpost]
    if mix is not None:
        oa, ob, gates, wa, wb, wo, _ = mix
        in_specs += [
            pl.BlockSpec((tm, oa.shape[1]), row),
            pl.BlockSpec((tm, ob.shape[1]), row),
            pl.BlockSpec((tm, gates.shape[1]), row),
            pl.BlockSpec(wa.shape, const, **resident),
            pl.BlockSpec(wb.shape, const, **resident),
            pl.BlockSpec(wo.shape, const, **resident),
            pl.BlockSpec((1, D), const),
        ]
        args += list(mix)
    if ple is not None:
        p, _, w_gate, w_proj, _ = ple
        in_specs += [
            pl.BlockSpec((tm, p.shape[1]), row),
            pl.BlockSpec((1, D), const),
            pl.BlockSpec(w_gate.shape, const, **resident),
            pl.BlockSpec(w_proj.shape, const, **resident),
            pl.BlockSpec((1, D), const),
        ]
        args += list(ple)
    return pl.pallas_call(
        functools.partial(_ffn_kernel, splits=splits, with_mix=mix is not None, with_ple=ple is not None),
        grid=(T // tm,),
        in_specs=in_specs,
        out_specs=pl.BlockSpec((tm, D), row),
        out_shape=jax.ShapeDtypeStruct((T, D), F32),
        compiler_params=pltpu.CompilerParams(
            dimension_semantics=("parallel",), vmem_limit_bytes=VMEM_LIMIT),
        name="ffn" + ("_mix" if mix is not None else "") + ("_ple" if ple is not None else ""),
    )(*args)


def _rope(x, c, sa, sb):
    half = ROT_DIM // 2
    return x * c + pltpu.roll(x, LANES - half, 1) * sa + pltpu.roll(x, half, 1) * sb


def _mix_in_kernel(h_ref, gpre_ref, w_ref, c_ref, sa_ref, sb_ref, kig_ref, smscale_ref,
                   qkva_ref, z_ref, qb_ref, kb_ref, vb_ref, qi_ref, g_ref, small_ref):
    xn = (_rms(h_ref[...]) * gpre_ref[...]).astype(BF16)
    c, sa, sb = c_ref[...], sa_ref[...], sb_ref[...]
    W = GROUP_W

    def proj(lo, hi):
        return _dot(xn, w_ref[:, lo:hi])

    qkva_ref[...] = proj(0, 3 * W)
    z_ref[...] = proj(3 * W, 4 * W)

    def roped(lo, scale):
        y = proj(lo, lo + W)
        parts = [_rope(y[:, t * LANES:(t + 1) * LANES], c, sa, sb) for t in range(W // LANES)]
        return (jnp.concatenate(parts, axis=1) * scale).astype(BF16)

    inv_sqrt_d = HEAD_DIM ** -0.5
    qb_ref[...] = roped(4 * W, inv_sqrt_d * LOG2_E)
    kb_ref[...] = roped(5 * W, 1.0)
    vb_ref[...] = proj(6 * W, 7 * W).astype(BF16)
    qi_ref[...] = roped(7 * W, inv_sqrt_d)
    g_ref[...] = proj(8 * W, 12 * W).astype(BF16)

    sm = proj(12 * W, 12 * W + LANES)
    lane = lax.broadcasted_iota(I32, sm.shape, 1)
    is_ki = lane < HEAD_DIM
    ms = jnp.sum(jnp.where(is_ki, sm * sm, 0.0), axis=-1, keepdims=True) * (1.0 / HEAD_DIM)
    ki = _rope(sm * lax.rsqrt(ms + EPS) * kig_ref[...], c, sa, sb)
    small_ref[...] = jnp.where(is_ki, ki, sm * smscale_ref[...])


def _mix_in(h, gpre, w, rope_c, rope_sa, rope_sb, kig, smscale, *, tm, seq):
    T, D = h.shape
    W = GROUP_W
    ncol = w.shape[1]
    nseq = seq // tm
    row = lambda i: (i, 0)
    const = lambda i: (0, 0)
    pos = lambda i: (i % nseq, 0)
    outs = [(3 * W, F32), (W, F32), (W, BF16), (W, BF16), (W, BF16), (W, BF16), (4 * W, BF16), (LANES, F32)]
    return pl.pallas_call(
        _mix_in_kernel,
        grid=(T // tm,),
        in_specs=[
            pl.BlockSpec((tm, D), row),
            pl.BlockSpec((1, D), const),
            pl.BlockSpec((D, ncol), const, pipeline_mode=pl.Buffered(1)),
            pl.BlockSpec((tm, LANES), pos),
            pl.BlockSpec((tm, LANES), pos),
            pl.BlockSpec((tm, LANES), pos),
            pl.BlockSpec((1, LANES), const),
            pl.BlockSpec((1, LANES), const),
        ],
        out_specs=[pl.BlockSpec((tm, n), row) for n, _ in outs],
        out_shape=[jax.ShapeDtypeStruct((T, n), dt) for n, dt in outs],
        compiler_params=pltpu.CompilerParams(
            dimension_semantics=("parallel",), vmem_limit_bytes=VMEM_LIMIT),
        name="mix_in",
    )(h, gpre, w, rope_c, rope_sa, rope_sb, kig, smscale)


def _gdn_kernel(qkv_ref, z_ref, small_ref, convw_ref, alog_ref, dtb_ref, alogc_ref, dtbc_ref,
                dng_ref, bd_ref, o_ref,
                xs_ref, q3_ref, k3_ref, v3_ref, pw_ref, qkd_ref, sol_ref, u_ref, wd_ref,
                qd_ref, kdT_ref, oc_ref, state_ref, *, G, P, C):
    W = GROUP_W
    HD = HEAD_DIM
    H = N_HEADS
    U = G * H
    shift = int(math.log2(C))
    i = pl.program_id(1)

    @pl.when(i == 0)
    def _():
        for g in range(G):
            xs_ref[g, 0:8, :] = jnp.zeros((8, 3 * W), F32)
        state_ref[...] = jnp.zeros_like(state_ref)

    bd = bd_ref[...]

    def head_sum(x):
        return _dot(x.astype(BF16), bd)

    ri = lax.broadcasted_iota(I32, (P, P), 0)
    ci = lax.broadcasted_iota(I32, (P, P), 1)
    same = lax.shift_right_logical(ri, shift) == lax.shift_right_logical(ci, shift)
    incl = jnp.logical_and(same, ri >= ci)
    diag = ri == ci
    tri = jnp.where(incl, 1.0, 0.0).astype(BF16)
    triT = jnp.where(same, jnp.where(ci >= ri, 1.0, 0.0), 0.0).astype(BF16)
    blk = jnp.where(same, 1.0, 0.0).astype(BF16)

    gc, gcT, bfull, egc, ekd = [], [], [], [], []
    for g in range(G):
        xs_ref[g, 8:8 + P, :] = qkv_ref[g]
        w = convw_ref[...]
        y = w[CONV_TAPS - 1:CONV_TAPS, :] * xs_ref[g, 8:8 + P, :]
        for tap in range(CONV_TAPS - 1):
            off = 8 - (CONV_TAPS - 1) + tap
            y = y + w[tap:tap + 1, :] * xs_ref[g, off:off + P, :]
        xs_ref[g, 0:8, :] = xs_ref[g, P:P + 8, :]
        y = y * _sigmoid(y)
        q, k, v = y[:, :W], y[:, W:2 * W], y[:, 2 * W:]
        qn = q * lax.rsqrt(head_sum(q * q) + EPS) * (HD ** -0.5)
        kn = k * lax.rsqrt(head_sum(k * k) + EPS)
        for h in range(H):
            sl = slice(h * HD, (h + 1) * HD)
            q3_ref[g * H + h] = qn[:, sl]
            k3_ref[g * H + h] = kn[:, sl]
            v3_ref[g * H + h] = v[:, sl]

        sm = small_ref[g]
        gfull = -jnp.exp(alog_ref[...]) * _softplus(sm + dtb_ref[...])
        gT = -jnp.exp(alogc_ref[...]) * _softplus(sm.T[SM_A:SM_A + H, :] + dtbc_ref[...])
        g_parts = _split3(gfull)
        gc_g = sum(_dot(tri, part) for part in g_parts)
        gtot = sum(_dot(blk, part) for part in g_parts)
        gc.append(gc_g)
        gcT.append(sum(_dot(part, triT) for part in _split3(gT)))
        bfull.append(_sigmoid(sm))
        egc.append(jnp.exp(gc_g))
        ekd.append(jnp.exp(gtot - gc_g))

    for u in range(U):
        g, h = divmod(u, H)
        K, Q, V = k3_ref[u], q3_ref[u], v3_ref[u]
        Kb = K.astype(BF16)
        gcol = gc[g][:, SM_A + h:SM_A + h + 1]
        grow = gcT[g][h:h + 1, :]
        bcol = bfull[g][:, SM_B + h:SM_B + h + 1]
        ecol = egc[g][:, SM_A + h:SM_A + h + 1]
        decay = jnp.where(incl, jnp.exp(gcol - grow), 0.0)
        n_mat = jnp.where(diag, 0.0, -(bcol * _dot_nt(Kb, Kb) * decay)).astype(BF16)
        pw_ref[0, u] = n_mat
        qkd_ref[u] = (_dot_nt(Q.astype(BF16), Kb) * decay).astype(BF16)
        rhs = jnp.concatenate([V * bcol, K * (bcol * ecol)], axis=1)
        sol_ref[u] = rhs + _dot(n_mat, rhs.astype(BF16))
        qd_ref[u] = (Q * ecol).astype(BF16)
        kdT_ref[u] = (K * ekd[g][:, SM_A + h:SM_A + h + 1]).T.astype(BF16)

    for it in range(shift - 1):
        src, dst = it % 2, 1 - it % 2
        for u in range(U):
            pw = pw_ref[src, u]
            pw_ref[dst, u] = _dot(pw, pw).astype(BF16)
        for u in range(U):
            y = sol_ref[u]
            sol_ref[u] = y + _dot(pw_ref[dst, u], y.astype(BF16))
    for u in range(U):
        sol = sol_ref[u]
        u_ref[u] = sol[:, :HD]
        wd_ref[u] = sol[:, HD:].astype(BF16)

    for c in range(P // C):
        rows = slice(c * C, (c + 1) * C)
        states = [state_ref[u] for u in range(U)]
        sbf = [s.astype(BF16) for s in states]
        v_new = [(u_ref[u, rows, :] - _dot(wd_ref[u, rows, :], sbf[u])).astype(BF16) for u in range(U)]
        for u in range(U):
            g, h = divmod(u, H)
            oc_ref[g, rows, h * HD:(h + 1) * HD] = (
                _dot(qd_ref[u, rows, :], sbf[u]) + _dot(qkd_ref[u, rows, rows], v_new[u]))
        for u in range(U):
            g, h = divmod(u, H)
            g_tot = jnp.exp(gcT[g][h:h + 1, (c + 1) * C - 1:(c + 1) * C])
            state_ref[u] = states[u] * g_tot + _dot(kdT_ref[u, :, rows], v_new[u])

    for g in range(G):
        o = oc_ref[g]
        ms = head_sum(o * o) * (1.0 / HD)
        zz = z_ref[g]
        o_ref[g] = (o * lax.rsqrt(ms + EPS) * dng_ref[...] * (zz * _sigmoid(zz))).astype(BF16)


def _gdn(qkva, z, small, convw, alog_row, dtb_row, alog_col, dtb_col, dng, bd, *, batch, seq, G, P, C):
    W = GROUP_W
    H, HD = N_HEADS, HEAD_DIM
    U = G * H
    per_seq = lambda a: a.reshape(batch, seq, a.shape[-1])
    row = lambda b, i: (b, i, 0)
    const = lambda b, i: (0, 0)
    out = pl.pallas_call(
        functools.partial(_gdn_kernel, G=G, P=P, C=C),
        grid=(batch // G, seq // P),
        in_specs=[
            pl.BlockSpec((G, P, 3 * W), row),
            pl.BlockSpec((G, P, W), row),
            pl.BlockSpec((G, P, LANES), row),
            pl.BlockSpec((CONV_TAPS, 3 * W), const),
            pl.BlockSpec((1, LANES), const),
            pl.BlockSpec((1, LANES), const),
            pl.BlockSpec((H, 1), const),
            pl.BlockSpec((H, 1), const),
            pl.BlockSpec((1, W), const),
            pl.BlockSpec((W, W), const),
        ],
        out_specs=pl.BlockSpec((G, P, W), row),
        out_shape=jax.ShapeDtypeStruct((batch, seq, W), BF16),
        scratch_shapes=[
            pltpu.VMEM((G, P + 8, 3 * W), F32),
            pltpu.VMEM((U, P, HD), F32),
            pltpu.VMEM((U, P, HD), F32),
            pltpu.VMEM((U, P, HD), F32),
            pltpu.VMEM((2, U, P, P), BF16),
            pltpu.VMEM((U, P, P), BF16),
            pltpu.VMEM((U, P, 2 * HD), F32),
            pltpu.VMEM((U, P, HD), F32),
            pltpu.VMEM((U, P, HD), BF16),
            pltpu.VMEM((U, P, HD), BF16),
            pltpu.VMEM((U, HD, P), BF16),
            pltpu.VMEM((G, P, W), F32),
            pltpu.VMEM((U, HD, HD), F32),
        ],
        compiler_params=pltpu.CompilerParams(
            dimension_semantics=("parallel", "arbitrary"), vmem_limit_bytes=VMEM_LIMIT),
        name="gdn",
    )(per_seq(qkva), per_seq(z), per_seq(small), convw, alog_row, dtb_row, alog_col, dtb_col, dng, bd)
    return out.reshape(batch * seq, W)


def _key_to_f32(key):
    bits = key ^ (lax.shift_right_arithmetic(key, 31) & 0x7FFFFFFF)
    return pltpu.bitcast(bits, F32)


def _f32_to_key(x):
    bits = pltpu.bitcast(x, I32)
    return bits ^ (lax.shift_right_arithmetic(bits, 31) & 0x7FFFFFFF)


def _dsa_kernel(qb_ref, qi_ref, wsm_ref, kb_ref, vb_ref, ksm_ref, o_ref,
                sc_ref, vT_ref, kib_ref, s_ref, acc_ref, knorm_ref, *, Tq, Tk, seq, topk):
    HD = HEAD_DIM
    H = N_HEADS
    W = GROUP_W
    ACC = 2 * SUBLANES
    i = pl.program_id(1)
    nkt = ((i + 1) * Tq + Tk - 1) // Tk
    qpos = i * Tq + lax.broadcasted_iota(I32, (1, Tq), 1)
    sub_pos = lax.broadcasted_iota(I32, (Tk, Tq), 0)
    head_rows = lax.broadcasted_iota(I32, (H, W), 0)
    head_of_lane = lax.shift_right_logical(lax.broadcasted_iota(I32, (H, W), 1), int(math.log2(HD)))
    head_sel = jnp.where(head_rows == head_of_lane, 1.0, 0.0).astype(BF16)

    @pl.when(i == 0)
    def _():
        k_norm2 = jnp.zeros((H, 1), F32)
        for kt in range(seq // Tk):
            rows = slice(kt * Tk, (kt + 1) * Tk)
            vT_ref[kt] = vb_ref[rows, :].astype(F32).T.astype(BF16)
            k_tile = kb_ref[rows, :].astype(F32)
            per_head = _dot_nt(head_sel, (k_tile * k_tile).astype(BF16))
            k_norm2 = jnp.maximum(k_norm2, jnp.max(per_head, axis=1, keepdims=True))
        knorm_ref[...] = jnp.broadcast_to(jnp.sqrt(k_norm2), knorm_ref.shape)
        kib_ref[...] = ksm_ref[...].astype(BF16)

    wT = wsm_ref[...].T
    qi_heads = [qi_ref[:, h * HD:(h + 1) * HD] for h in range(H)]
    w_rows = [wT[SM_WI + h:SM_WI + h + 1, :] for h in range(H)]

    def fold(m):
        return jnp.sum(m.reshape(m.shape[0] // ACC, ACC, Tq), axis=0)

    def score_body(kt, carry):
        rmax, rmin, ge0, gt0 = carry
        r0 = pl.multiple_of(kt * Tk, Tk)
        ki = kib_ref[pl.ds(r0, Tk), :][:, SM_KI:SM_KI + HD]
        dots = [_dot_nt(ki, qi_heads[h]) for h in range(H)]
        score = jnp.zeros((Tk, Tq), F32)
        for h in range(H):
            score = score + w_rows[h] * jnp.maximum(dots[h], 0.0)
        adm = r0 + sub_pos <= qpos
        sc = jnp.where(adm, score, -jnp.inf)
        sc_ref[pl.ds(r0, Tk), :] = sc
        rmax = jnp.maximum(rmax, jnp.max(sc, axis=0, keepdims=True))
        rmin = jnp.minimum(rmin, jnp.min(jnp.where(adm, score, jnp.inf), axis=0, keepdims=True))
        ge0 = ge0 + fold(jnp.where(sc >= 0.0, 1.0, 0.0))
        gt0 = gt0 + fold(jnp.where(sc > 0.0, 1.0, 0.0))
        return rmax, rmin, ge0, gt0

    row = lambda v: jnp.full((1, Tq), v, F32)
    rmax, rmin, ge0, gt0 = lax.fori_loop(
        0, nkt, score_body, (row(-jnp.inf), row(jnp.inf), jnp.zeros((ACC, Tq), F32), jnp.zeros((ACC, Tq), F32)))
    cnt_ge0 = jnp.sum(ge0, axis=0, keepdims=True)
    cnt_gt0 = jnp.sum(gt0, axis=0, keepdims=True)

    def count(pred):
        def body(j, acc):
            r0 = pl.multiple_of(j * COUNT_ROWS, COUNT_ROWS)
            return acc + fold(pred(r0, sc_ref[pl.ds(r0, COUNT_ROWS), :]))
        acc = lax.fori_loop(0, nkt * (Tk // COUNT_ROWS), body, jnp.zeros((ACC, Tq), F32))
        return jnp.sum(acc, axis=0, keepdims=True)

    def count_ge(v):
        return count(lambda r0, s: jnp.where(s >= v, 1.0, 0.0))

    def count_gt(v):
        return count(lambda r0, s: jnp.where(s > v, 1.0, 0.0))

    kf = float(topk)
    log_k = math.log(kf)
    short = qpos < topk
    zero_tie = jnp.logical_and(cnt_gt0 < kf, cnt_ge0 >= kf)
    positive = cnt_gt0 >= kf
    tiny = row(FLT_MIN_NORMAL)
    lo_k = jnp.where(positive, _f32_to_key(tiny), _f32_to_key(rmin))
    hi_k = jnp.where(positive, _f32_to_key(rmax) + 1, _f32_to_key(-tiny))
    c_lo = jnp.where(positive, cnt_gt0, (qpos + 1).astype(F32))
    c_hi = jnp.where(positive, 0.0, cnt_ge0)
    done0 = jnp.where(jnp.logical_or(jnp.logical_or(short, zero_tie), c_lo == kf), 1.0, 0.0)

    def probe(st):
        it, lo_k, hi_k, c_lo, c_hi, w_lo, w_hi, last, done = st
        lo = _key_to_f32(lo_k)
        hi = _key_to_f32(hi_k - 1)
        a = (jnp.log(c_lo + 0.5) - log_k) * w_lo
        b = (log_k - jnp.log(c_hi + 0.5)) * w_hi
        frac = jnp.clip(a / jnp.maximum(a + b, 1e-30), 0.02, 0.98)
        cand_k = _f32_to_key(lo + frac * (hi - lo))
        mid_k = lo_k + lax.shift_right_logical(hi_k - lo_k, 1)
        cand_k = jnp.where(jnp.logical_and(it >= SEARCH_FAST_PASSES, it % 4 == 3), mid_k, cand_k)
        cand_k = jnp.minimum(jnp.maximum(cand_k, lo_k + 1), hi_k - 1)
        c = count_ge(_key_to_f32(cand_k))
        live = done == 0.0
        up = jnp.logical_and(live, c >= kf)
        dn = jnp.logical_and(live, c < kf)
        w_hi = jnp.where(jnp.logical_and(up, last > 0.0), w_hi * 0.5, jnp.where(dn, 1.0, w_hi))
        w_lo = jnp.where(jnp.logical_and(dn, last < 0.0), w_lo * 0.5, jnp.where(up, 1.0, w_lo))
        last = jnp.where(up, 1.0, jnp.where(dn, -1.0, last))
        lo_k = jnp.where(up, cand_k, lo_k)
        c_lo = jnp.where(up, c, c_lo)
        hi_k = jnp.where(dn, cand_k, hi_k)
        c_hi = jnp.where(dn, c, c_hi)
        done = jnp.where(jnp.logical_or(c_lo == kf, hi_k - lo_k <= 1), 1.0, done)
        return it + 1, lo_k, hi_k, c_lo, c_hi, w_lo, w_hi, last, done

    def search_cond(st):
        far = jnp.where(jnp.logical_and(st[-1] == 0.0, kf - st[4] > 1.0), 1.0, 0.0)
        return jnp.logical_and(st[0] < SEARCH_MAX_PASSES, jnp.max(far) > 0.0)

    st = (jnp.int32(0), lo_k, hi_k, c_lo, c_hi, row(1.0), row(1.0), row(0.0), done0)
    st = lax.while_loop(search_cond, lambda st: probe(probe(st)), st)
    _, lo_k, hi_k, c_lo, c_hi = st[:5]
    open_rows = st[-1] == 0.0

    lo_f = jnp.where(short, -jnp.inf, jnp.where(zero_tie, 0.0, _key_to_f32(lo_k)))
    hi_f = jnp.where(zero_tie, tiny, _key_to_f32(hi_k))
    c_lo = jnp.where(zero_tie, cnt_ge0, c_lo)
    c_hi = jnp.where(zero_tie, cnt_gt0, c_hi)

    def below_body(kt, best):
        r0 = pl.multiple_of(kt * Tk, Tk)
        s = sc_ref[pl.ds(r0, Tk), :]
        cand = jnp.where(s < hi_f, s, -jnp.inf)
        return jnp.maximum(best, jnp.max(cand.reshape(Tk // ACC, ACC, Tq), axis=0))

    below = jnp.max(lax.fori_loop(0, nkt, below_body, jnp.full((ACC, Tq), -jnp.inf, F32)),
                    axis=0, keepdims=True)
    lo_f = jnp.where(open_rows, below, lo_f)
    c_lo = jnp.where(open_rows, c_hi + 1.0, c_lo)
    cnt_gt = count_gt(lo_f)

    def unresolved(lo_f, hi_f, c_lo, c_hi, cnt_gt):
        inside = c_lo - c_hi
        u = jnp.logical_and(kf - c_hi < inside, c_lo - cnt_gt < inside)
        return jnp.where(short, 0.0, jnp.where(u, 1.0, 0.0))

    def refine_cond(st):
        return jnp.logical_and(st[0] < SEARCH_MAX_PASSES, jnp.max(unresolved(*st[1:])) > 0.0)

    def refine(st):
        it, lo_f, hi_f, c_lo, c_hi, cnt_gt = st
        live = unresolved(lo_f, hi_f, c_lo, c_hi, cnt_gt) > 0.0
        mid = 0.5 * lo_f + 0.5 * hi_f
        c = count_ge(mid)
        up = jnp.logical_and(live, c >= kf)
        dn = jnp.logical_and(live, c < kf)
        lo_f = jnp.where(up, mid, lo_f)
        c_lo = jnp.where(up, c, c_lo)
        hi_f = jnp.where(dn, mid, hi_f)
        c_hi = jnp.where(dn, c, c_hi)
        return it + 1, lo_f, hi_f, c_lo, c_hi, count_gt(lo_f)

    _, thr, _, _, _, cnt_gt = lax.while_loop(
        refine_cond, refine, (jnp.int32(0), lo_f, hi_f, c_lo, c_hi, cnt_gt))

    quota = jnp.where(short, 0.0, kf - cnt_gt)
    before = jnp.where(lax.broadcasted_iota(I32, (Tk, Tk), 0) > lax.broadcasted_iota(I32, (Tk, Tk), 1),
                       1.0, 0.0).astype(BF16)

    def mask_body(kt, seen):
        r0 = pl.multiple_of(kt * Tk, Tk)
        sc = sc_ref[pl.ds(r0, Tk), :]
        eq = jnp.where(sc == thr, 1.0, 0.0)
        rank = _dot(before, eq.astype(BF16)) + seen
        tie = jnp.where(rank < quota, 0.0, -jnp.inf)
        sc_ref[pl.ds(r0, Tk), :] = jnp.where(sc > thr, 0.0, jnp.where(sc == thr, tie, -jnp.inf))
        return seen + jnp.sum(eq, axis=0, keepdims=True)

    lax.fori_loop(0, nkt, mask_body, jnp.zeros((1, Tq), F32))

    q_all = qb_ref[...].astype(F32)
    q_norm = jnp.sqrt(_dot_nt(head_sel, (q_all * q_all).astype(BF16)))
    shift = [q_norm[h:h + 1, :] * (ATT_BOUND_MARGIN * knorm_ref[h:h + 1, 0:1]) for h in range(H)]

    def att_fast(kt, l_all):
        r0 = pl.multiple_of(kt * Tk, Tk)
        bias = sc_ref[pl.ds(r0, Tk), :]
        for h in range(H):
            sl = slice(h * HD, (h + 1) * HD)
            s_ref[h] = _dot_nt(kb_ref[pl.ds(r0, Tk), sl], qb_ref[:, sl]) + bias - shift[h]
        l_rows = []
        for h in range(H):
            p = jnp.exp2(s_ref[h])
            l_rows.append(l_all[h:h + 1, :] + jnp.sum(p, axis=0, keepdims=True))
            acc_ref[h] += _dot(vT_ref[kt, h * HD:(h + 1) * HD, :], p.astype(BF16))
        return jnp.concatenate(l_rows, axis=0)

    acc_ref[...] = jnp.zeros_like(acc_ref)
    l_fast = lax.fori_loop(0, nkt, att_fast, jnp.zeros((H, Tq), F32))

    def att_slow(kt, carry):
        m_all, l_all = carry
        r0 = pl.multiple_of(kt * Tk, Tk)
        bias = sc_ref[pl.ds(r0, Tk), :]
        tile_max = []
        for h in range(H):
            sl = slice(h * HD, (h + 1) * HD)
            s = _dot_nt(kb_ref[pl.ds(r0, Tk), sl], qb_ref[:, sl]) + bias
            s_ref[h] = s
            tile_max.append(jnp.max(s, axis=0, keepdims=True))
        m_rows, l_rows = [], []
        for h in range(H):
            s = s_ref[h]
            m_old = m_all[h:h + 1, :]
            m_new = jnp.maximum(m_old, tile_max[h])
            alpha = jnp.exp2(m_old - m_new)
            p = jnp.exp2(s - m_new)
            l_rows.append(alpha * l_all[h:h + 1, :] + jnp.sum(p, axis=0, keepdims=True))
            acc_ref[h] = alpha * acc_ref[h] + _dot(vT_ref[kt, h * HD:(h + 1) * HD, :], p.astype(BF16))
            m_rows.append(m_new)
        return jnp.concatenate(m_rows, axis=0), jnp.concatenate(l_rows, axis=0)

    def redo():
        acc_ref[...] = jnp.zeros_like(acc_ref)
        init = (jnp.full((H, Tq), -1e30, F32), jnp.zeros((H, Tq), F32))
        return lax.fori_loop(0, nkt, att_slow, init)[1]

    l_fin = lax.cond(jnp.min(l_fast) > ATT_MIN_DENOM, lambda: l_fast, redo)
    out_t = jnp.concatenate([acc_ref[h] / l_fin[h:h + 1, :] for h in range(H)], axis=0)
    o_ref[...] = out_t.T.astype(BF16)


def _dsa(qb, kb, vb, qi, small, *, batch, seq, Tq, Tk):
    T = qb.shape[0]
    W = GROUP_W
    nq = seq // Tq
    qrow = lambda b, i: (b * nq + i, 0)
    full = lambda b, i: (b, 0)
    topk = min(TOPK_MAX, seq // 4)
    return pl.pallas_call(
        functools.partial(_dsa_kernel, Tq=Tq, Tk=Tk, seq=seq, topk=topk),
        grid=(batch, nq),
        in_specs=[
            pl.BlockSpec((Tq, W), qrow),
            pl.BlockSpec((Tq, W), qrow),
            pl.BlockSpec((Tq, LANES), qrow),
            pl.BlockSpec((seq, W), full),
            pl.BlockSpec((seq, W), full),
            pl.BlockSpec((seq, LANES), full),
        ],
        out_specs=pl.BlockSpec((Tq, W), qrow),
        out_shape=jax.ShapeDtypeStruct((T, W), BF16),
        scratch_shapes=[
            pltpu.VMEM((seq, Tq), F32),
            pltpu.VMEM((seq // Tk, W, Tk), BF16),
            pltpu.VMEM((seq, LANES), BF16),
            pltpu.VMEM((N_HEADS, Tk, Tq), F32),
            pltpu.VMEM((N_HEADS, HEAD_DIM, Tq), F32),
            pltpu.VMEM((N_HEADS, LANES), F32),
        ],
        compiler_params=pltpu.CompilerParams(
            dimension_semantics=("parallel", "arbitrary"), vmem_limit_bytes=VMEM_LIMIT),
        name="dsa",
    )(qb, qi, small, kb, vb, small)


def _rope_tables(seq):
    half = ROT_DIM // 2
    pos = jnp.arange(seq, dtype=F32)
    inv = ROPE_THETA ** (-jnp.arange(0, ROT_DIM, 2, dtype=F32) / ROT_DIM)
    ang = pos[:, None] * inv[None, :]
    cos, sin = jnp.cos(ang), jnp.sin(ang)
    ones = jnp.ones((seq, HEAD_DIM - ROT_DIM), F32)
    zeros_h = jnp.zeros((seq, half), F32)
    zeros_r = jnp.zeros((seq, HEAD_DIM - ROT_DIM), F32)
    c = jnp.concatenate([cos, cos, ones], axis=1)
    sa = jnp.concatenate([-sin, zeros_h, zeros_r], axis=1)
    sb = jnp.concatenate([zeros_h, sin, zeros_r], axis=1)
    rep = LANES // HEAD_DIM
    return jnp.tile(c, (1, rep)), jnp.tile(sa, (1, rep)), jnp.tile(sb, (1, rep))


def _pad_lanes(vec, offset):
    out = jnp.zeros((1, LANES), F32)
    return lax.dynamic_update_slice(out, vec.reshape(1, -1).astype(F32), (0, offset))


def kernel(x, p, ffn1_norm_pre, ffn1_norm_post, ffn1_w_in, ffn1_w_out, mix_norm_pre, mix_norm_post, mix_w_in, conv_w, a_log, dt_bias, dn_norm_g, idx_k_norm_g, w_br_a, w_br_b, mix_w_out, ffn2_norm_pre, ffn2_norm_post, ffn2_w_in, ffn2_w_out, ple_norm_pre, ple_norm_post, ple_w_gate, ple_w_proj):
    B, S, D = x.shape
    T = B * S
    W = GROUP_W
    depth = ffn1_w_in.shape[0]

    tm_ffn = min(512, T)
    tm_mix = min(512, S)
    gdn_rows = min(128, S)
    gdn_seqs = 2 if B % 2 == 0 else 1
    gdn_chunk = min(64, gdn_rows)
    Tq = min(512, S)
    Tk = min(512, S)

    rope_c, rope_sa, rope_sb = _rope_tables(S)
    ri = jnp.arange(W)
    bd = (ri[:, None] // HEAD_DIM == ri[None, :] // HEAD_DIM).astype(BF16)
    smscale = _pad_lanes(jnp.full((N_HEADS,), N_HEADS ** -0.5, F32), SM_WI)
    smscale = jnp.where(smscale == 0.0, 1.0, smscale)
    vec = lambda a: a.reshape(1, -1).astype(F32)

    h = x.reshape(T, D)
    for l in range(depth):
        h = _ffn(h, vec(ffn1_norm_pre[l]), ffn1_w_in[l].astype(BF16), ffn1_w_out[l].astype(BF16),
                 vec(ffn1_norm_post[l]), tm=tm_ffn, n_split=2)

        wi = mix_w_in[l]
        o_a, o_b = 4 * W, 4 * W + 2 * N_HEADS
        o_ki = o_b + 4 * W
        o_wi = o_ki + HEAD_DIM
        o_g = o_wi + N_HEADS
        w_cat = jnp.concatenate([
            wi[:, :4 * W], wi[:, o_b:o_ki], wi[:, o_g:o_g + 2 * D],
            wi[:, o_ki:o_g], wi[:, o_a:o_b],
            jnp.zeros((D, LANES - HEAD_DIM - 3 * N_HEADS), wi.dtype)], axis=1).astype(BF16)
        qkva, z, qb, kb, vb, qi, g, small = _mix_in(
            h, vec(mix_norm_pre[l]), w_cat, rope_c, rope_sa, rope_sb,
            _pad_lanes(idx_k_norm_g[l], SM_KI), smscale, tm=tm_mix, seq=S)

        o_gdn = _gdn(qkva, z, small, conv_w[l].astype(F32),
                     _pad_lanes(a_log[l], SM_A), _pad_lanes(dt_bias[l], SM_A),
                     a_log[l].reshape(N_HEADS, 1).astype(F32), dt_bias[l].reshape(N_HEADS, 1).astype(F32),
                     jnp.tile(vec(dn_norm_g[l]), (1, N_HEADS)), bd,
                     batch=B, seq=S, G=gdn_seqs, P=gdn_rows, C=gdn_chunk)
        o_dsa = _dsa(qb, kb, vb, qi, small, batch=B, seq=S, Tq=Tq, Tk=Tk)

        mix = (o_gdn, o_dsa, g, w_br_a[l].astype(BF16), w_br_b[l].astype(BF16),
               mix_w_out[l].astype(BF16), vec(mix_norm_post[l]))
        ple = (p[l].reshape(T, -1), vec(ple_norm_pre[l]), ple_w_gate[l].astype(BF16),
               ple_w_proj[l].astype(BF16), vec(ple_norm_post[l]))
        h = _ffn(h, vec(ffn2_norm_pre[l]), ffn2_w_in[l].astype(BF16), ffn2_w_out[l].astype(BF16),
                 vec(ffn2_norm_post[l]), mix, ple, tm=tm_ffn, n_split=2)
    return h.reshape(B, S, D)
```

```python
import functools
import math

import jax
import jax.numpy as jnp
from jax import lax
from jax.experimental import pallas as pl
from jax.experimental.pallas import tpu as pltpu

F32 = jnp.float32
BF16 = jnp.bfloat16
I32 = jnp.int32

EPS = 1e-6
N_HEADS = 8
HEAD_DIM = 64
GROUP_W = N_HEADS * HEAD_DIM
CONV_TAPS = 4
ROT_DIM = HEAD_DIM // 4
ROPE_THETA = 500000.0
TOPK_MAX = 256
LANES = 128
SUBLANES = 8
MXU_DEPTH = 256
FLT_MIN_NORMAL = 1.1754943508222875e-38
LOG2_E = 1.4426950408889634
SEARCH_FAST_PASSES = 24
SEARCH_MAX_PASSES = 256
COUNT_ROWS = 128
ATT_BOUND_MARGIN = 1.02
ATT_MIN_DENOM = 1e-30
VMEM_LIMIT = 56 * 1024 * 1024

SM_KI = 0
SM_WI = 64
SM_A = 72
SM_B = 80


def _rms(x):
    return x * lax.rsqrt(jnp.mean(x * x, axis=-1, keepdims=True) + EPS)


def _dot(a, b):
    return jnp.dot(a, b, preferred_element_type=F32)


def _dot_nt(a, b):
    return lax.dot_general(a, b, (((1,), (1,)), ((), ())), preferred_element_type=F32)


def _split3(x):
    hi = x.astype(BF16)
    r = x - hi.astype(F32)
    mid = r.astype(BF16)
    lo = (r - mid.astype(F32)).astype(BF16)
    return hi, mid, lo


def _softplus(x):
    return jnp.maximum(x, 0.0) + jnp.log(1.0 + jnp.exp(-jnp.abs(x)))


def _sigmoid(x):
    return 1.0 / (1.0 + jnp.exp(-x))


def _ffn_kernel(x_ref, gpre_ref, w_in_ref, w_out_ref, gpost_ref, *rest, splits, with_mix, with_ple):
    o_ref = rest[-1]
    rest = rest[:-1]
    ff = w_out_ref.shape[0]
    x = x_ref[...]
    if with_mix:
        oa_ref, ob_ref, g_ref, wa_ref, wb_ref, wo_ref, mpost_ref = rest[:7]
        rest = rest[7:]
        D = x.shape[1]
        gates = g_ref[...].astype(F32)
        merged = (_sigmoid(gates[:, :D]) * _dot(oa_ref[...], wa_ref[...])
                  + _sigmoid(gates[:, D:]) * _dot(ob_ref[...], wb_ref[...]))
        x = x + _rms(_dot(merged.astype(BF16), wo_ref[...])) * mpost_ref[...]
    xn = (_rms(x) * gpre_ref[...]).astype(BF16)
    acc = None
    for lo, hi in splits:
        g = _dot(xn, w_in_ref[:, lo:hi])
        u = _dot(xn, w_in_ref[:, ff + lo:ff + hi])
        part = _dot((g * _sigmoid(g) * u).astype(BF16), w_out_ref[lo:hi, :])
        acc = part if acc is None else acc + part
    h = x + 0.5 * (_rms(acc) * gpost_ref[...])
    if with_ple:
        p_ref, pre_ref, wg_ref, wp_ref, post_ref = rest
        gate = _sigmoid(_dot((_rms(h) * pre_ref[...]).astype(BF16), wg_ref[...]))
        e = _dot(p_ref[...].astype(BF16), wp_ref[...])
        h = h + _rms(gate * e) * post_ref[...]
    o_ref[...] = h


def _ffn(h, gpre, w_in, w_out, gpost, mix=None, ple=None, *, tm, n_split):
    T, D = h.shape
    FF = w_out.shape[0]
    step = -(-FF // (n_split * MXU_DEPTH)) * MXU_DEPTH
    splits = tuple((lo, min(lo + step, FF)) for lo in range(0, FF, step))
    resident = dict(pipeline_mode=pl.Buffered(1))
    row = lambda i: (i, 0)
    const = lambda i: (0, 0)
    in_specs = [
        pl.BlockSpec((tm, D), row),
        pl.BlockSpec((1, D), const),
        pl.BlockSpec((D, 2 * FF), const, **resident),
        pl.BlockSpec((FF, D), const, **resident),
        pl.BlockSpec((1, D), const),
    ]
    args = [h, gpre, w_in, w_out, gpost]
    if mix is not None:
        oa, ob, gates, wa, wb, wo, _ = mix
        in_specs += [
            pl.BlockSpec((tm, oa.shape[1]), row),
            pl.BlockSpec((tm, ob.shape[1]), row),
            pl.BlockSpec((tm, gates.shape[1]), row),
            pl.BlockSpec(wa.shape, const, **resident),
            pl.BlockSpec(wb.shape, const, **resident),
            pl.BlockSpec(wo.shape, const, **resident),
            pl.BlockSpec((1, D), const),
        ]
        args += list(mix)
    if ple is not None:
        p, _, w_gate, w_proj, _ = ple
        in_specs += [
            pl.BlockSpec((tm, p.shape[1]), row),
            pl.BlockSpec((1, D), const),
            pl.BlockSpec(w_gate.shape, const, **resident),
            pl.BlockSpec(w_proj.shape, const, **resident),
            pl.BlockSpec((1, D), const),
        ]
        args += list(ple)
    return pl.pallas_call(
        functools.partial(_ffn_kernel, splits=splits, with_mix=mix is not None, with_ple=ple is not None),
        grid=(T // tm,),
        in_specs=in_specs,
        out_specs=pl.BlockSpec((tm, D), row),
        out_shape=jax.ShapeDtypeStruct((T, D), F32),
        compiler_params=pltpu.CompilerParams(
            dimension_semantics=("parallel",), vmem_limit_bytes=VMEM_LIMIT),
        name="ffn" + ("_mix" if mix is not None else "") + ("_ple" if ple is not None else ""),
    )(*args)


def _rope(x, c, sa, sb):
    half = ROT_DIM // 2
    return x * c + pltpu.roll(x, LANES - half, 1) * sa + pltpu.roll(x, half, 1) * sb


def _mix_in_kernel(h_ref, gpre_ref, w_ref, c_ref, sa_ref, sb_ref, kig_ref, smscale_ref,
                   qkva_ref, z_ref, qb_ref, kb_ref, vb_ref, qi_ref, g_ref, small_ref):
    xn = (_rms(h_ref[...]) * gpre_ref[...]).astype(BF16)
    c, sa, sb = c_ref[...], sa_ref[...], sb_ref[...]
    W = GROUP_W

    def proj(lo, hi):
        return _dot(xn, w_ref[:, lo:hi])

    qkva_ref[...] = proj(0, 3 * W)
    z_ref[...] = proj(3 * W, 4 * W)

    def roped(lo, scale):
        y = proj(lo, lo + W)
        parts = [_rope(y[:, t * LANES:(t + 1) * LANES], c, sa, sb) for t in range(W // LANES)]
        return (jnp.concatenate(parts, axis=1) * scale).astype(BF16)

    inv_sqrt_d = HEAD_DIM ** -0.5
    qb_ref[...] = roped(4 * W, inv_sqrt_d * LOG2_E)
    kb_ref[...] = roped(5 * W, 1.0)
    vb_ref[...] = proj(6 * W, 7 * W).astype(BF16)
    qi_ref[...] = roped(7 * W, inv_sqrt_d)
    g_ref[...] = proj(8 * W, 12 * W).astype(BF16)

    sm = proj(12 * W, 12 * W + LANES)
    lane = lax.broadcasted_iota(I32, sm.shape, 1)
    is_ki = lane < HEAD_DIM
    ms = jnp.sum(jnp.where(is_ki, sm * sm, 0.0), axis=-1, keepdims=True) * (1.0 / HEAD_DIM)
    ki = _rope(sm * lax.rsqrt(ms + EPS) * kig_ref[...], c, sa, sb)
    small_ref[...] = jnp.where(is_ki, ki, sm * smscale_ref[...])


def _mix_in(h, gpre, w, rope_c, rope_sa, rope_sb, kig, smscale, *, tm, seq):
    T, D = h.shape
    W = GROUP_W
    ncol = w.shape[1]
    nseq = seq // tm
    row = lambda i: (i, 0)
    const = lambda i: (0, 0)
    pos = lambda i: (i % nseq, 0)
    outs = [(3 * W, F32), (W, F32), (W, BF16), (W, BF16), (W, BF16), (W, BF16), (4 * W, BF16), (LANES, F32)]
    return pl.pallas_call(
        _mix_in_kernel,
        grid=(T // tm,),
        in_specs=[
            pl.BlockSpec((tm, D), row),
            pl.BlockSpec((1, D), const),
            pl.BlockSpec((D, ncol), const, pipeline_mode=pl.Buffered(1)),
            pl.BlockSpec((tm, LANES), pos),
            pl.BlockSpec((tm, LANES), pos),
            pl.BlockSpec((tm, LANES), pos),
            pl.BlockSpec((1, LANES), const),
            pl.BlockSpec((1, LANES), const),
        ],
        out_specs=[pl.BlockSpec((tm, n), row) for n, _ in outs],
        out_shape=[jax.ShapeDtypeStruct((T, n), dt) for n, dt in outs],
        compiler_params=pltpu.CompilerParams(
            dimension_semantics=("parallel",), vmem_limit_bytes=VMEM_LIMIT),
        name="mix_in",
    )(h, gpre, w, rope_c, rope_sa, rope_sb, kig, smscale)


def _gdn_kernel(qkv_ref, z_ref, small_ref, convw_ref, alog_ref, dtb_ref, alogc_ref, dtbc_ref,
                dng_ref, bd_ref, o_ref,
                xs_ref, q3_ref, k3_ref, v3_ref, pw_ref, qkd_ref, sol_ref, u_ref, wd_ref,
                qd_ref, kdT_ref, oc_ref, state_ref, *, G, P, C):
    W = GROUP_W
    HD = HEAD_DIM
    H = N_HEADS
    U = G * H
    shift = int(math.log2(C))
    i = pl.program_id(1)

    @pl.when(i == 0)
    def _():
        for g in range(G):
            xs_ref[g, 0:8, :] = jnp.zeros((8, 3 * W), F32)
        state_ref[...] = jnp.zeros_like(state_ref)

    bd = bd_ref[...]

    def head_sum(x):
        return _dot(x.astype(BF16), bd)

    ri = lax.broadcasted_iota(I32, (P, P), 0)
    ci = lax.broadcasted_iota(I32, (P, P), 1)
    same = lax.shift_right_logical(ri, shift) == lax.shift_right_logical(ci, shift)
    incl = jnp.logical_and(same, ri >= ci)
    diag = ri == ci
    tri = jnp.where(incl, 1.0, 0.0).astype(BF16)
    triT = jnp.where(same, jnp.where(ci >= ri, 1.0, 0.0), 0.0).astype(BF16)
    blk = jnp.where(same, 1.0, 0.0).astype(BF16)

    gc, gcT, bfull, egc, ekd = [], [], [], [], []
    for g in range(G):
        xs_ref[g, 8:8 + P, :] = qkv_ref[g]
        w = convw_ref[...]
        y = w[CONV_TAPS - 1:CONV_TAPS, :] * xs_ref[g, 8:8 + P, :]
        for tap in range(CONV_TAPS - 1):
            off = 8 - (CONV_TAPS - 1) + tap
            y = y + w[tap:tap + 1, :] * xs_ref[g, off:off + P, :]
        xs_ref[g, 0:8, :] = xs_ref[g, P:P + 8, :]
        y = y * _sigmoid(y)
        q, k, v = y[:, :W], y[:, W:2 * W], y[:, 2 * W:]
        qn = q * lax.rsqrt(head_sum(q * q) + EPS) * (HD ** -0.5)
        kn = k * lax.rsqrt(head_sum(k * k) + EPS)
        for h in range(H):
            sl = slice(h * HD, (h + 1) * HD)
            q3_ref[g * H + h] = qn[:, sl]
            k3_ref[g * H + h] = kn[:, sl]
            v3_ref[g * H + h] = v[:, sl]

        sm = small_ref[g]
        gfull = -jnp.exp(alog_ref[...]) * _softplus(sm + dtb_ref[...])
        gT = -jnp.exp(alogc_ref[...]) * _softplus(sm.T[SM_A:SM_A + H, :] + dtbc_ref[...])
        g_parts = _split3(gfull)
        gc_g = sum(_dot(tri, part) for part in g_parts)
        gtot = sum(_dot(blk, part) for part in g_parts)
        gc.append(gc_g)
        gcT.append(sum(_dot(part, triT) for part in _split3(gT)))
        bfull.append(_sigmoid(sm))
        egc.append(jnp.exp(gc_g))
        ekd.append(jnp.exp(gtot - gc_g))

    for u in range(U):
        g, h = divmod(u, H)
        K, Q, V = k3_ref[u], q3_ref[u], v3_ref[u]
        Kb = K.astype(BF16)
        gcol = gc[g][:, SM_A + h:SM_A + h + 1]
        grow = gcT[g][h:h + 1, :]
        bcol = bfull[g][:, SM_B + h:SM_B + h + 1]
        ecol = egc[g][:, SM_A + h:SM_A + h + 1]
        decay = jnp.where(incl, jnp.exp(gcol - grow), 0.0)
        n_mat = jnp.where(diag, 0.0, -(bcol * _dot_nt(Kb, Kb) * decay)).astype(BF16)
        pw_ref[0, u] = n_mat
        qkd_ref[u] = (_dot_nt(Q.astype(BF16), Kb) * decay).astype(BF16)
        rhs = jnp.concatenate([V * bcol, K * (bcol * ecol)], axis=1)
        sol_ref[u] = rhs + _dot(n_mat, rhs.astype(BF16))
        qd_ref[u] = (Q * ecol).astype(BF16)
        kdT_ref[u] = (K * ekd[g][:, SM_A + h:SM_A + h + 1]).astype(BF16)

    for it in range(shift - 1):
        src, dst = it % 2, 1 - it % 2
        for u in range(U):
            pw = pw_ref[src, u]
            pw_ref[dst, u] = _dot(pw, pw).astype(BF16)
        for u in range(U):
            y = sol_ref[u]
            sol_ref[u] = y + _dot(pw_ref[dst, u], y.astype(BF16))
    for u in range(U):
        sol = sol_ref[u]
        u_ref[u] = sol[:, :HD]
        wd_ref[u] = sol[:, HD:].astype(BF16)

    for c in range(P // C):
        rows = slice(c * C, (c + 1) * C)
        states = [state_ref[u] for u in range(U)]
        sbf = [s.astype(BF16) for s in states]
        v_new = [(u_ref[u, rows, :] - _dot(wd_ref[u, rows, :], sbf[u])).astype(BF16) for u in range(U)]
        for u in range(U):
            g, h = divmod(u, H)
            oc_ref[g, rows, h * HD:(h + 1) * HD] = (
                _dot(qd_ref[u, rows, :], sbf[u]) + _dot(qkd_ref[u, rows, rows], v_new[u]))
        for u in range(U):
            g, h = divmod(u, H)
            g_tot = jnp.exp(gcT[g][h:h + 1, (c + 1) * C - 1:(c + 1) * C])
            state_ref[u] = states[u] * g_tot + lax.dot_general(kdT_ref[u, rows, :], v_new[u], (((0,), (0,)), ((), ())), preferred_element_type=F32)

    for g in range(G):
        o = oc_ref[g]
        ms = head_sum(o * o) * (1.0 / HD)
        zz = z_ref[g]
        o_ref[g] = (o * lax.rsqrt(ms + EPS) * dng_ref[...] * (zz * _sigmoid(zz))).astype(BF16)


def _gdn(qkva, z, small, convw, alog_row, dtb_row, alog_col, dtb_col, dng, bd, *, batch, seq, G, P, C):
    W = GROUP_W
    H, HD = N_HEADS, HEAD_DIM
    U = G * H
    per_seq = lambda a: a.reshape(batch, seq, a.shape[-1])
    row = lambda b, i: (b, i, 0)
    const = lambda b, i: (0, 0)
    out = pl.pallas_call(
        functools.partial(_gdn_kernel, G=G, P=P, C=C),
        grid=(batch // G, seq // P),
        in_specs=[
            pl.BlockSpec((G, P, 3 * W), row),
            pl.BlockSpec((G, P, W), row),
            pl.BlockSpec((G, P, LANES), row),
            pl.BlockSpec((CONV_TAPS, 3 * W), const),
            pl.BlockSpec((1, LANES), const),
            pl.BlockSpec((1, LANES), const),
            pl.BlockSpec((H, 1), const),
            pl.BlockSpec((H, 1), const),
            pl.BlockSpec((1, W), const),
            pl.BlockSpec((W, W), const),
        ],
        out_specs=pl.BlockSpec((G, P, W), row),
        out_shape=jax.ShapeDtypeStruct((batch, seq, W), BF16),
        scratch_shapes=[
            pltpu.VMEM((G, P + 8, 3 * W), F32),
            pltpu.VMEM((U, P, HD), F32),
            pltpu.VMEM((U, P, HD), F32),
            pltpu.VMEM((U, P, HD), F32),
            pltpu.VMEM((2, U, P, P), BF16),
            pltpu.VMEM((U, P, P), BF16),
            pltpu.VMEM((U, P, 2 * HD), F32),
            pltpu.VMEM((U, P, HD), F32),
            pltpu.VMEM((U, P, HD), BF16),
            pltpu.VMEM((U, P, HD), BF16),
            pltpu.VMEM((U, P, HD), BF16),
            pltpu.VMEM((G, P, W), F32),
            pltpu.VMEM((U, HD, HD), F32),
        ],
        compiler_params=pltpu.CompilerParams(
            dimension_semantics=("parallel", "arbitrary"), vmem_limit_bytes=VMEM_LIMIT),
        name="gdn",
    )(per_seq(qkva), per_seq(z), per_seq(small), convw, alog_row, dtb_row, alog_col, dtb_col, dng, bd)
    return out.reshape(batch * seq, W)


def _key_to_f32(key):
    bits = key ^ (lax.shift_right_arithmetic(key, 31) & 0x7FFFFFFF)
    return pltpu.bitcast(bits, F32)


def _f32_to_key(x):
    bits = pltpu.bitcast(x, I32)
    return bits ^ (lax.shift_right_arithmetic(bits, 31) & 0x7FFFFFFF)


def _dsa_kernel(qb_ref, qi_ref, wsm_ref, kb_ref, vb_ref, ksm_ref, o_ref,
                sc_ref, vT_ref, kib_ref, s_ref, acc_ref, knorm_ref, *, Tq, Tk, seq, topk):
    HD = HEAD_DIM
    H = N_HEADS
    W = GROUP_W
    ACC = 2 * SUBLANES
    i = pl.program_id(1)
    nkt = ((i + 1) * Tq + Tk - 1) // Tk
    qpos = i * Tq + lax.broadcasted_iota(I32, (1, Tq), 1)
    sub_pos = lax.broadcasted_iota(I32, (Tk, Tq), 0)
    head_rows = lax.broadcasted_iota(I32, (H, W), 0)
    head_of_lane = lax.shift_right_logical(lax.broadcasted_iota(I32, (H, W), 1), int(math.log2(HD)))
    head_sel = jnp.where(head_rows == head_of_lane, 1.0, 0.0).astype(BF16)

    @pl.when(i == 0)
    def _():
        k_norm2 = jnp.zeros((H, 1), F32)
        for kt in range(seq // Tk):
            rows = slice(kt * Tk, (kt + 1) * Tk)
            vT_ref[kt] = vb_ref[rows, :].astype(F32).T.astype(BF16)
            k_tile = kb_ref[rows, :].astype(F32)
            per_head = _dot_nt(head_sel, (k_tile * k_tile).astype(BF16))
            k_norm2 = jnp.maximum(k_norm2, jnp.max(per_head, axis=1, keepdims=True))
        knorm_ref[...] = jnp.broadcast_to(jnp.sqrt(k_norm2), knorm_ref.shape)
        kib_ref[...] = ksm_ref[...].astype(BF16)

    wT = wsm_ref[...].T
    qi_heads = [qi_ref[:, h * HD:(h + 1) * HD] for h in range(H)]
    w_rows = [wT[SM_WI + h:SM_WI + h + 1, :] for h in range(H)]

    def fold(m):
        return jnp.sum(m.reshape(m.shape[0] // ACC, ACC, Tq), axis=0)

    def score_body(kt, carry):
        rmax, rmin, ge0, gt0 = carry
        r0 = pl.multiple_of(kt * Tk, Tk)
        ki = kib_ref[pl.ds(r0, Tk), :][:, SM_KI:SM_KI + HD]
        dots = [_dot_nt(ki, qi_heads[h]) for h in range(H)]
        score = jnp.zeros((Tk, Tq), F32)
        for h in range(H):
            score = score + w_rows[h] * jnp.maximum(dots[h], 0.0)
        adm = r0 + sub_pos <= qpos
        sc = jnp.where(adm, score, -jnp.inf)
        sc_ref[pl.ds(r0, Tk), :] = sc
        rmax = jnp.maximum(rmax, jnp.max(sc, axis=0, keepdims=True))
        rmin = jnp.minimum(rmin, jnp.min(jnp.where(adm, score, jnp.inf), axis=0, keepdims=True))
        ge0 = ge0 + fold(jnp.where(sc >= 0.0, 1.0, 0.0))
        gt0 = gt0 + fold(jnp.where(sc > 0.0, 1.0, 0.0))
        return rmax, rmin, ge0, gt0

    row = lambda v: jnp.full((1, Tq), v, F32)
    rmax, rmin, ge0, gt0 = lax.fori_loop(
        0, nkt, score_body, (row(-jnp.inf), row(jnp.inf), jnp.zeros((ACC, Tq), F32), jnp.zeros((ACC, Tq), F32)))
    cnt_ge0 = jnp.sum(ge0, axis=0, keepdims=True)
    cnt_gt0 = jnp.sum(gt0, axis=0, keepdims=True)

    def count(pred):
        def body(j, acc):
            r0 = pl.multiple_of(j * COUNT_ROWS, COUNT_ROWS)
            return acc + fold(pred(r0, sc_ref[pl.ds(r0, COUNT_ROWS), :]))
        acc = lax.fori_loop(0, nkt * (Tk // COUNT_ROWS), body, jnp.zeros((ACC, Tq), F32))
        return jnp.sum(acc, axis=0, keepdims=True)

    def count_ge(v):
        return count(lambda r0, s: jnp.where(s >= v, 1.0, 0.0))

    def count_gt(v):
        return count(lambda r0, s: jnp.where(s > v, 1.0, 0.0))

    kf = float(topk)
    log_k = math.log(kf)
    short = qpos < topk
    zero_tie = jnp.logical_and(cnt_gt0 < kf, cnt_ge0 >= kf)
    positive = cnt_gt0 >= kf
    tiny = row(FLT_MIN_NORMAL)
    lo_k = jnp.where(positive, _f32_to_key(tiny), _f32_to_key(rmin))
    hi_k = jnp.where(positive, _f32_to_key(rmax) + 1, _f32_to_key(-tiny))
    c_lo = jnp.where(positive, cnt_gt0, (qpos + 1).astype(F32))
    c_hi = jnp.where(positive, 0.0, cnt_ge0)
    done0 = jnp.where(jnp.logical_or(jnp.logical_or(short, zero_tie), c_lo == kf), 1.0, 0.0)

    def probe(st):
        it, lo_k, hi_k, c_lo, c_hi, w_lo, w_hi, last, done = st
        lo = _key_to_f32(lo_k)
        hi = _key_to_f32(hi_k - 1)
        a = (jnp.log(c_lo + 0.5) - log_k) * w_lo
        b = (log_k - jnp.log(c_hi + 0.5)) * w_hi
        frac = jnp.clip(a / jnp.maximum(a + b, 1e-30), 0.02, 0.98)
        cand_k = _f32_to_key(lo + frac * (hi - lo))
        mid_k = lo_k + lax.shift_right_logical(hi_k - lo_k, 1)
        cand_k = jnp.where(jnp.logical_and(it >= SEARCH_FAST_PASSES, it % 4 == 3), mid_k, cand_k)
        cand_k = jnp.minimum(jnp.maximum(cand_k, lo_k + 1), hi_k - 1)
        c = count_ge(_key_to_f32(cand_k))
        live = done == 0.0
        up = jnp.logical_and(live, c >= kf)
        dn = jnp.logical_and(live, c < kf)
        w_hi = jnp.where(jnp.logical_and(up, last > 0.0), w_hi * 0.5, jnp.where(dn, 1.0, w_hi))
        w_lo = jnp.where(jnp.logical_and(dn, last < 0.0), w_lo * 0.5, jnp.where(up, 1.0, w_lo))
        last = jnp.where(up, 1.0, jnp.where(dn, -1.0, last))
        lo_k = jnp.where(up, cand_k, lo_k)
        c_lo = jnp.where(up, c, c_lo)
        hi_k = jnp.where(dn, cand_k, hi_k)
        c_hi = jnp.where(dn, c, c_hi)
        done = jnp.where(jnp.logical_or(c_lo == kf, hi_k - lo_k <= 1), 1.0, done)
        return it + 1, lo_k, hi_k, c_lo, c_hi, w_lo, w_hi, last, done

    def search_cond(st):
        far = jnp.where(jnp.logical_and(st[-1] == 0.0, kf - st[4] > 1.0), 1.0, 0.0)
        return jnp.logical_and(st[0] < SEARCH_MAX_PASSES, jnp.max(far) > 0.0)

    st = (jnp.int32(0), lo_k, hi_k, c_lo, c_hi, row(1.0), row(1.0), row(0.0), done0)
    st = lax.while_loop(search_cond, lambda st: probe(probe(st)), st)
    _, lo_k, hi_k, c_lo, c_hi = st[:5]
    open_rows = st[-1] == 0.0

    lo_f = jnp.where(short, -jnp.inf, jnp.where(zero_tie, 0.0, _key_to_f32(lo_k)))
    hi_f = jnp.where(zero_tie, tiny, _key_to_f32(hi_k))
    c_lo = jnp.where(zero_tie, cnt_ge0, c_lo)
    c_hi = jnp.where(zero_tie, cnt_gt0, c_hi)

    def below_body(kt, best):
        r0 = pl.multiple_of(kt * Tk, Tk)
        s = sc_ref[pl.ds(r0, Tk), :]
        cand = jnp.where(s < hi_f, s, -jnp.inf)
        return jnp.maximum(best, jnp.max(cand.reshape(Tk // ACC, ACC, Tq), axis=0))

    below = jnp.max(lax.fori_loop(0, nkt, below_body, jnp.full((ACC, Tq), -jnp.inf, F32)),
                    axis=0, keepdims=True)
    lo_f = jnp.where(open_rows, below, lo_f)
    c_lo = jnp.where(open_rows, c_hi + 1.0, c_lo)
    cnt_gt = count_gt(lo_f)

    def unresolved(lo_f, hi_f, c_lo, c_hi, cnt_gt):
        inside = c_lo - c_hi
        u = jnp.logical_and(kf - c_hi < inside, c_lo - cnt_gt < inside)
        return jnp.where(short, 0.0, jnp.where(u, 1.0, 0.0))

    def refine_cond(st):
        return jnp.logical_and(st[0] < SEARCH_MAX_PASSES, jnp.max(unresolved(*st[1:])) > 0.0)

    def refine(st):
        it, lo_f, hi_f, c_lo, c_hi, cnt_gt = st
        live = unresolved(lo_f, hi_f, c_lo, c_hi, cnt_gt) > 0.0
        mid = 0.5 * lo_f + 0.5 * hi_f
        c = count_ge(mid)
        up = jnp.logical_and(live, c >= kf)
        dn = jnp.logical_and(live, c < kf)
        lo_f = jnp.where(up, mid, lo_f)
        c_lo = jnp.where(up, c, c_lo)
        hi_f = jnp.where(dn, mid, hi_f)
        c_hi = jnp.where(dn, c, c_hi)
        return it + 1, lo_f, hi_f, c_lo, c_hi, count_gt(lo_f)

    _, thr, _, _, _, cnt_gt = lax.while_loop(
        refine_cond, refine, (jnp.int32(0), lo_f, hi_f, c_lo, c_hi, cnt_gt))

    quota = jnp.where(short, 0.0, kf - cnt_gt)
    before = jnp.where(lax.broadcasted_iota(I32, (Tk, Tk), 0) > lax.broadcasted_iota(I32, (Tk, Tk), 1),
                       1.0, 0.0).astype(BF16)

    def mask_body(kt, seen):
        r0 = pl.multiple_of(kt * Tk, Tk)
        sc = sc_ref[pl.ds(r0, Tk), :]
        eq = jnp.where(sc == thr, 1.0, 0.0)
        rank = _dot(before, eq.astype(BF16)) + seen
        tie = jnp.where(rank < quota, 0.0, -jnp.inf)
        sc_ref[pl.ds(r0, Tk), :] = jnp.where(sc > thr, 0.0, jnp.where(sc == thr, tie, -jnp.inf))
        return seen + jnp.sum(eq, axis=0, keepdims=True)

    lax.fori_loop(0, nkt, mask_body, jnp.zeros((1, Tq), F32))

    q_all = qb_ref[...].astype(F32)
    q_norm = jnp.sqrt(_dot_nt(head_sel, (q_all * q_all).astype(BF16)))
    shift = [q_norm[h:h + 1, :] * (ATT_BOUND_MARGIN * knorm_ref[h:h + 1, 0:1]) for h in range(H)]

    def att_fast(kt, l_all):
        r0 = pl.multiple_of(kt * Tk, Tk)
        bias = sc_ref[pl.ds(r0, Tk), :]
        for h in range(H):
            sl = slice(h * HD, (h + 1) * HD)
            s_ref[h] = _dot_nt(kb_ref[pl.ds(r0, Tk), sl], qb_ref[:, sl]) + bias - shift[h]
        l_rows = []
        for h in range(H):
            p = jnp.exp2(s_ref[h])
            l_rows.append(l_all[h:h + 1, :] + jnp.sum(p, axis=0, keepdims=True))
            acc_ref[h] += _dot(vT_ref[kt, h * HD:(h + 1) * HD, :], p.astype(BF16))
        return jnp.concatenate(l_rows, axis=0)

    acc_ref[...] = jnp.zeros_like(acc_ref)
    l_fast = lax.fori_loop(0, nkt, att_fast, jnp.zeros((H, Tq), F32))

    def att_slow(kt, carry):
        m_all, l_all = carry
        r0 = pl.multiple_of(kt * Tk, Tk)
        bias = sc_ref[pl.ds(r0, Tk), :]
        tile_max = []
        for h in range(H):
            sl = slice(h * HD, (h + 1) * HD)
            s = _dot_nt(kb_ref[pl.ds(r0, Tk), sl], qb_ref[:, sl]) + bias
            s_ref[h] = s
            tile_max.append(jnp.max(s, axis=0, keepdims=True))
        m_rows, l_rows = [], []
        for h in range(H):
            s = s_ref[h]
            m_old = m_all[h:h + 1, :]
            m_new = jnp.maximum(m_old, tile_max[h])
            alpha = jnp.exp2(m_old - m_new)
            p = jnp.exp2(s - m_new)
            l_rows.append(alpha * l_all[h:h + 1, :] + jnp.sum(p, axis=0, keepdims=True))
            acc_ref[h] = alpha * acc_ref[h] + _dot(vT_ref[kt, h * HD:(h + 1) * HD, :], p.astype(BF16))
            m_rows.append(m_new)
        return jnp.concatenate(m_rows, axis=0), jnp.concatenate(l_rows, axis=0)

    def redo():
        acc_ref[...] = jnp.zeros_like(acc_ref)
        init = (jnp.full((H, Tq), -1e30, F32), jnp.zeros((H, Tq), F32))
        return lax.fori_loop(0, nkt, att_slow, init)[1]

    l_fin = lax.cond(jnp.min(l_fast) > ATT_MIN_DENOM, lambda: l_fast, redo)
    out_t = jnp.concatenate([acc_ref[h] / l_fin[h:h + 1, :] for h in range(H)], axis=0)
    o_ref[...] = out_t.T.astype(BF16)


def _dsa(qb, kb, vb, qi, small, *, batch, seq, Tq, Tk):
    T = qb.shape[0]
    W = GROUP_W
    nq = seq // Tq
    qrow = lambda b, i: (b * nq + i, 0)
    full = lambda b, i: (b, 0)
    topk = min(TOPK_MAX, seq // 4)
    return pl.pallas_call(
        functools.partial(_dsa_kernel, Tq=Tq, Tk=Tk, seq=seq, topk=topk),
        grid=(batch, nq),
        in_specs=[
            pl.BlockSpec((Tq, W), qrow),
            pl.BlockSpec((Tq, W), qrow),
            pl.BlockSpec((Tq, LANES), qrow),
            pl.BlockSpec((seq, W), full),
            pl.BlockSpec((seq, W), full),
            pl.BlockSpec((seq, LANES), full),
        ],
        out_specs=pl.BlockSpec((Tq, W), qrow),
        out_shape=jax.ShapeDtypeStruct((T, W), BF16),
        scratch_shapes=[
            pltpu.VMEM((seq, Tq), F32),
            pltpu.VMEM((seq // Tk, W, Tk), BF16),
            pltpu.VMEM((seq, LANES), BF16),
            pltpu.VMEM((N_HEADS, Tk, Tq), F32),
            pltpu.VMEM((N_HEADS, HEAD_DIM, Tq), F32),
            pltpu.VMEM((N_HEADS, LANES), F32),
        ],
        compiler_params=pltpu.CompilerParams(
            dimension_semantics=("parallel", "arbitrary"), vmem_limit_bytes=VMEM_LIMIT),
        name="dsa",
    )(qb, qi, small, kb, vb, small)


def _rope_tables(seq):
    half = ROT_DIM // 2
    pos = jnp.arange(seq, dtype=F32)
    inv = ROPE_THETA ** (-jnp.arange(0, ROT_DIM, 2, dtype=F32) / ROT_DIM)
    ang = pos[:, None] * inv[None, :]
    cos, sin = jnp.cos(ang), jnp.sin(ang)
    ones = jnp.ones((seq, HEAD_DIM - ROT_DIM), F32)
    zeros_h = jnp.zeros((seq, half), F32)
    zeros_r = jnp.zeros((seq, HEAD_DIM - ROT_DIM), F32)
    c = jnp.concatenate([cos, cos, ones], axis=1)
    sa = jnp.concatenate([-sin, zeros_h, zeros_r], axis=1)
    sb = jnp.concatenate([zeros_h, sin, zeros_r], axis=1)
    rep = LANES // HEAD_DIM
    return jnp.tile(c, (1, rep)), jnp.tile(sa, (1, rep)), jnp.tile(sb, (1, rep))


def _pad_lanes(vec, offset):
    out = jnp.zeros((1, LANES), F32)
    return lax.dynamic_update_slice(out, vec.reshape(1, -1).astype(F32), (0, offset))


def kernel(x, p, ffn1_norm_pre, ffn1_norm_post, ffn1_w_in, ffn1_w_out, mix_norm_pre, mix_norm_post, mix_w_in, conv_w, a_log, dt_bias, dn_norm_g, idx_k_norm_g, w_br_a, w_br_b, mix_w_out, ffn2_norm_pre, ffn2_norm_post, ffn2_w_in, ffn2_w_out, ple_norm_pre, ple_norm_post, ple_w_gate, ple_w_proj):
    B, S, D = x.shape
    T = B * S
    W = GROUP_W
    depth = ffn1_w_in.shape[0]

    tm_ffn = min(512, T)
    tm_mix = min(512, S)
    gdn_rows = min(128, S)
    gdn_seqs = 2 if B % 2 == 0 else 1
    gdn_chunk = min(64, gdn_rows)
    Tq = min(512, S)
    Tk = min(512, S)

    rope_c, rope_sa, rope_sb = _rope_tables(S)
    ri = jnp.arange(W)
    bd = (ri[:, None] // HEAD_DIM == ri[None, :] // HEAD_DIM).astype(BF16)
    smscale = _pad_lanes(jnp.full((N_HEADS,), N_HEADS ** -0.5, F32), SM_WI)
    smscale = jnp.where(smscale == 0.0, 1.0, smscale)
    vec = lambda a: a.reshape(1, -1).astype(F32)

    h = x.reshape(T, D)
    for l in range(depth):
        h = _ffn(h, vec(ffn1_norm_pre[l]), ffn1_w_in[l].astype(BF16), ffn1_w_out[l].astype(BF16),
                 vec(ffn1_norm_post[l]), tm=tm_ffn, n_split=2)

        wi = mix_w_in[l]
        o_a, o_b = 4 * W, 4 * W + 2 * N_HEADS
        o_ki = o_b + 4 * W
        o_wi = o_ki + HEAD_DIM
        o_g = o_wi + N_HEADS
        w_cat = jnp.concatenate([
            wi[:, :4 * W], wi[:, o_b:o_ki], wi[:, o_g:o_g + 2 * D],
            wi[:, o_ki:o_g], wi[:, o_a:o_b],
            jnp.zeros((D, LANES - HEAD_DIM - 3 * N_HEADS), wi.dtype)], axis=1).astype(BF16)
        qkva, z, qb, kb, vb, qi, g, small = _mix_in(
            h, vec(mix_norm_pre[l]), w_cat, rope_c, rope_sa, rope_sb,
            _pad_lanes(idx_k_norm_g[l], SM_KI), smscale, tm=tm_mix, seq=S)

        o_gdn = _gdn(qkva, z, small, conv_w[l].astype(F32),
                     _pad_lanes(a_log[l], SM_A), _pad_lanes(dt_bias[l], SM_A),
                     a_log[l].reshape(N_HEADS, 1).astype(F32), dt_bias[l].reshape(N_HEADS, 1).astype(F32),
                     jnp.tile(vec(dn_norm_g[l]), (1, N_HEADS)), bd,
                     batch=B, seq=S, G=gdn_seqs, P=gdn_rows, C=gdn_chunk)
        o_dsa = _dsa(qb, kb, vb, qi, small, batch=B, seq=S, Tq=Tq, Tk=Tk)

        mix = (o_gdn, o_dsa, g, w_br_a[l].astype(BF16), w_br_b[l].astype(BF16),
               mix_w_out[l].astype(BF16), vec(mix_norm_post[l]))
        ple = (p[l].reshape(T, -1), vec(ple_norm_pre[l]), ple_w_gate[l].astype(BF16),
               ple_w_proj[l].astype(BF16), vec(ple_norm_post[l]))
        h = _ffn(h, vec(ffn2_norm_pre[l]), ffn2_w_in[l].astype(BF16), ffn2_w_out[l].astype(BF16),
                 vec(ffn2_norm_post[l]), mix, ple, tm=tm_ffn, n_split=2)
    return h.reshape(B, S, D)
```
